```python
import math
import jax, jax.numpy as jnp
from jax import lax
import numpy as np

D_MODEL = 1024
BATCH = 8
SEQ = 4096
DEPTH = 4

CHUNK = 64
Q_BLOCK = 128
MIX_WIDTH = D_MODEL
ATT_WIDTH = MIX_WIDTH // 2
REC_WIDTH = MIX_WIDTH - ATT_WIDTH
H_A = 4
D_A = ATT_WIDTH // (2 * H_A)
H_R = 4
D_K = REC_WIDTH // H_R
D_V = REC_WIDTH // H_R
D_FF = 4 * D_MODEL
NUM_BUCKETS = 32
MAX_DISTANCE = 128
IN_COLS = 3 * ATT_WIDTH + 4 * REC_WIDTH
EPS = 1e-6
NEG_INF = -1e30

kernel_name = "hymba_diffattn_hgrn2_trunk"


def rms_norm(x, g):
    xf = x.astype(jnp.float32)
    y = xf * lax.rsqrt(jnp.mean(xf * xf, axis=-1, keepdims=True) + EPS)
    return (y * g.astype(jnp.float32)).astype(x.dtype)


def t5_bucket(rel):
    n_half = NUM_BUCKETS // 2
    max_exact = n_half // 2
    ret = jnp.where(rel > 0, n_half, 0)
    n = jnp.abs(rel)
    nf = jnp.maximum(n, 1).astype(jnp.float32)
    large = max_exact + (jnp.log(nf / max_exact) / math.log(MAX_DISTANCE / max_exact)
                         * (n_half - max_exact)).astype(jnp.int32)
    large = jnp.minimum(large, n_half - 1)
    return ret + jnp.where(n < max_exact, n, large)


def diff_attention(q1, q2, k1, k2, v, lam, rel_bias):
    B, H, S, d = q1.shape
    nblk = S // Q_BLOCK
    scale = d ** -0.5
    kpos = jnp.arange(S)

    def block(args):
        i, qa, qb = args
        qpos = i * Q_BLOCK + jnp.arange(Q_BLOCK)
        allowed = (kpos[None, :] // CHUNK) <= (qpos[:, None] // CHUNK)
        bias = rel_bias[t5_bucket(kpos[None, :] - qpos[:, None])]
        bias = jnp.transpose(bias, (2, 0, 1)).astype(jnp.float32)

        def probs(qx, kx):
            s = jnp.einsum('bhqd,bhkd->bhqk', qx, kx).astype(jnp.float32) * scale + bias
            s = jnp.where(allowed, s, NEG_INF)
            return jax.nn.softmax(s, axis=-1)

        p = probs(qa, k1) - lam * probs(qb, k2)
        return jnp.einsum('bhqk,bhkv->bhqv', p.astype(v.dtype), v)

    to_blocks = lambda t: t.reshape(B, H, nblk, Q_BLOCK, d).transpose(2, 0, 1, 3, 4)
    out = lax.map(block, (jnp.arange(nblk), to_blocks(q1), to_blocks(q2)))
    return out.transpose(1, 2, 0, 3, 4).reshape(B, H, S, v.shape[-1])


def hgrn2_chunked(q, g, k, v):
    B, H, S, dk = q.shape
    dv = v.shape[-1]
    nC = S // CHUNK
    to_chunks = lambda t: t.reshape(B, H, nC, CHUNK, t.shape[-1]).transpose(2, 0, 1, 3, 4)
    tri = jnp.tril(jnp.ones((CHUNK, CHUNK), dtype=bool))

    def step(state, inp):
        qc, gc, kc, vc = inp
        bc = jnp.cumsum(gc, axis=-2)
        inter = jnp.einsum('bhtk,bhkv->bhtv', qc * jnp.exp(bc), state)
        diff = bc[:, :, :, None, :] - bc[:, :, None, :, :]
        decay = jnp.exp(jnp.where(tri[:, :, None], diff, -jnp.inf))
        attn = jnp.einsum('bhtk,bhtsk,bhsk->bhts', qc, decay, kc)
        intra = jnp.einsum('bhts,bhsv->bhtv', attn, vc)
        blast = bc[:, :, -1:, :]
        state = (jnp.exp(blast[:, :, 0, :])[..., None] * state
                 + jnp.einsum('bhsk,bhsv->bhkv', kc * jnp.exp(blast - bc), vc))
        return state, inter + intra

    state0 = jnp.zeros((B, H, dk, dv), jnp.float32)
    _, out = lax.scan(step, state0, (to_chunks(q), to_chunks(g), to_chunks(k), to_chunks(v)))
    return out.transpose(1, 2, 0, 3, 4).reshape(B, H, S, dv)


def setup_inputs(seed: int = 0) -> dict:
    key = jax.random.key(seed)
    ks = jax.random.split(key, 16)
    nrm = lambda k, shape, s: jax.random.normal(k, shape, jnp.float32) * s
    return {
        "x": nrm(ks[0], (BATCH, SEQ, D_MODEL), 1.0),
        "norm1_g": 1.0 + nrm(ks[1], (DEPTH, D_MODEL), 0.02),
        "w_in": nrm(ks[2], (DEPTH, D_MODEL, IN_COLS), D_MODEL ** -0.5),
        "lam_qk": nrm(ks[3], (DEPTH, 4, D_A), 0.1),
        "attn_norm_g": 1.0 + nrm(ks[4], (DEPTH, 2 * D_A), 0.02),
        "lb_logits": nrm(ks[5], (DEPTH, REC_WIDTH), 0.1),
        "hgrn_norm_g": 1.0 + nrm(ks[6], (DEPTH, D_V), 0.02),
        "w_out": nrm(ks[7], (DEPTH, MIX_WIDTH, D_MODEL), MIX_WIDTH ** -0.5),
        "norm2_g": 1.0 + nrm(ks[8], (DEPTH, D_MODEL), 0.02),
        "w_up": nrm(ks[9], (DEPTH, D_MODEL, D_FF), D_MODEL ** -0.5),
        "w_down": nrm(ks[10], (DEPTH, D_FF, D_MODEL), D_FF ** -0.5),
        "rel_bias": nrm(ks[11], (NUM_BUCKETS, H_A), 0.5),
        "final_g": 1.0 + nrm(ks[12], (D_MODEL,), 0.02),
    }


def reference(x, norm1_g, w_in, lam_qk, attn_norm_g, lb_logits, hgrn_norm_g,
              w_out, norm2_g, w_up, w_down, rel_bias, final_g):
    B, S, _ = x.shape
    lb_all = jnp.cumsum(jax.nn.softmax(lb_logits.astype(jnp.float32), axis=0), axis=0)
    lb_all = lb_all - lb_all[0:1]

    for l in range(DEPTH):
        h = rms_norm(x, norm1_g[l])
        z = jnp.einsum('bsd,dc->bsc', h, w_in[l])
        aq, ak, av, rq, rf, ri, rg = jnp.split(
            z, np.cumsum([ATT_WIDTH] * 3 + [REC_WIDTH] * 3).tolist(), axis=-1)

        qh = aq.reshape(B, S, H_A, 2, D_A).transpose(0, 2, 1, 3, 4)
        kh = ak.reshape(B, S, H_A, 2, D_A).transpose(0, 2, 1, 3, 4)
        vh = av.reshape(B, S, H_A, 2 * D_A).transpose(0, 2, 1, 3)
        lam_init = 0.8 - 0.6 * math.exp(-0.3 * l)
        lq = lam_qk[l].astype(jnp.float32)
        lam = jnp.exp(jnp.sum(lq[0] * lq[1])) - jnp.exp(jnp.sum(lq[2] * lq[3])) + lam_init
        oa = diff_attention(qh[..., 0, :], qh[..., 1, :], kh[..., 0, :], kh[..., 1, :],
                            vh, lam, rel_bias)
        oa = rms_norm(oa, attn_norm_g[l]) * (1.0 - lam_init)
        oa = oa.transpose(0, 2, 1, 3).reshape(B, S, ATT_WIDTH)

        lb = lb_all[l].reshape(H_R, D_K)[None, :, None, :]
        to_heads = lambda t, d: t.reshape(B, S, H_R, d).transpose(0, 2, 1, 3).astype(jnp.float32)
        rf_h = to_heads(rf, D_K)
        log_f = jnp.logaddexp(jnp.log(lb), jnp.log1p(-lb) + jax.nn.log_sigmoid(rf_h))
        k_in = -jnp.expm1(log_f)
        q_r = jax.nn.silu(to_heads(rq, D_K))
        orr = hgrn2_chunked(q_r, log_f, k_in, to_heads(ri, D_V))
        orr = rms_norm(orr.transpose(0, 2, 1, 3), hgrn_norm_g[l]).reshape(B, S, REC_WIDTH)
        orr = orr.astype(x.dtype) * jax.nn.silu(rg)

        mixed = jnp.concatenate([oa.astype(x.dtype), orr], axis=-1)
        x = x + jnp.einsum('bsc,cd->bsd', mixed, w_out[l])

        h2 = rms_norm(x, norm2_g[l])
        u = jax.nn.relu(jnp.einsum('bsd,df->bsf', h2, w_up[l]))
        x = x + jnp.einsum('bsf,fd->bsd', u * u, w_down[l])

    return rms_norm(x, final_g)
```

```python
import functools
import math

import jax
import jax.numpy as jnp
from jax import lax
from jax.experimental import pallas as pl
from jax.experimental.pallas import tpu as pltpu

CHUNK = 64
H_A = 4
H_R = 4
NUM_BUCKETS = 32
MAX_DISTANCE = 128
EPS = 1e-6
NEG_INF = -1e30

LANES = 128
ROW_TILE = 512
COL_TILE = 512
ATT_TQ = 256
ATT_TK = 256
REC_TS = 512
SUB = 16
EXP_CLAMP = 80.0
VMEM_LIMIT = 56 * 1024 * 1024

F32 = jnp.float32
BF16 = jnp.bfloat16


def _rms(x, g):
    return x * lax.rsqrt(jnp.mean(x * x, axis=-1, keepdims=True) + EPS) * g


def _inproj_kernel(x_ref, g_ref, w_ref, za_ref, zr_ref, *, n_att):
    h = _rms(x_ref[...], g_ref[...]).astype(BF16)
    n_cols = w_ref.shape[1]
    for j in range(n_cols // COL_TILE):
        acc = jnp.dot(h, w_ref[:, j * COL_TILE:(j + 1) * COL_TILE],
                      preferred_element_type=F32)
        if j < n_att:
            za_ref[:, j * COL_TILE:(j + 1) * COL_TILE] = acc.astype(BF16)
        else:
            zr_ref[:, (j - n_att) * COL_TILE:(j - n_att + 1) * COL_TILE] = acc


def _inproj(x2d, g, w_bf16, att_cols):
    n, d = x2d.shape
    cols = w_bf16.shape[1]
    rec_cols = cols - att_cols
    return pl.pallas_call(
        functools.partial(_inproj_kernel, n_att=att_cols // COL_TILE),
        grid=(n // ROW_TILE,),
        in_specs=[
            pl.BlockSpec((ROW_TILE, d), lambda i: (i, 0)),
            pl.BlockSpec((1, d), lambda i: (0, 0)),
            pl.BlockSpec((d, cols), lambda i: (0, 0), pipeline_mode=pl.Buffered(1)),
        ],
        out_specs=[
            pl.BlockSpec((ROW_TILE, att_cols), lambda i: (i, 0)),
            pl.BlockSpec((ROW_TILE, rec_cols), lambda i: (i, 0)),
        ],
        out_shape=[
            jax.ShapeDtypeStruct((n, att_cols), BF16),
            jax.ShapeDtypeStruct((n, rec_cols), F32),
        ],
        compiler_params=pltpu.CompilerParams(
            dimension_semantics=("parallel",), vmem_limit_bytes=VMEM_LIMIT),
        name="inproj",
    )(x2d, g, w_bf16)


def _t5_bucket(rel):
    n_half = NUM_BUCKETS // 2
    max_exact = n_half // 2
    ret = jnp.where(rel > 0, n_half, 0)
    n = jnp.abs(rel)
    nf = jnp.maximum(n, 1).astype(jnp.float32)
    large = max_exact + (jnp.log(nf / max_exact) / math.log(MAX_DISTANCE / max_exact)
                         * (n_half - max_exact)).astype(jnp.int32)
    large = jnp.minimum(large, n_half - 1)
    return ret + jnp.where(n < max_exact, n, large)


def _bias_kernel(rb_ref, idx_ref, o_ref):
    h = pl.program_id(0)
    far = rb_ref[NUM_BUCKETS // 2 - 1, h]
    for t in range(2):
        idx = idx_ref[t]
        acc = jnp.zeros(idx.shape, F32)
        for b in range(NUM_BUCKETS):
            acc = jnp.where(idx == b, rb_ref[b, h] - far, acc)
        if t == 0:
            row = lax.broadcasted_iota(jnp.int32, idx.shape, 0)
            col = lax.broadcasted_iota(jnp.int32, idx.shape, 1)
            acc = jnp.where(col // CHUNK <= row // CHUNK, acc, NEG_INF)
        o_ref[0, t] = acc


def _bias_tiles(rel_bias):
    assert ATT_TQ == ATT_TK and ATT_TQ >= MAX_DISTANCE and ATT_TQ % CHUNK == 0
    qpos = jnp.arange(ATT_TQ)[:, None]
    kpos = jnp.arange(ATT_TK)[None, :]
    idx = jnp.stack([_t5_bucket(kpos - qpos), _t5_bucket(kpos - ATT_TK - qpos)]).astype(jnp.int32)
    return pl.pallas_call(
        _bias_kernel,
        grid=(H_A,),
        in_specs=[
            pl.BlockSpec(memory_space=pltpu.SMEM),
            pl.BlockSpec((2, ATT_TQ, ATT_TK), lambda h: (0, 0, 0)),
        ],
        out_specs=pl.BlockSpec((1, 2, ATT_TQ, ATT_TK), lambda h: (h, 0, 0, 0)),
        out_shape=jax.ShapeDtypeStruct((H_A, 2, ATT_TQ, ATT_TK), F32),
        name="bias_tiles",
    )(rel_bias.astype(F32), idx)


def _attn_kernel(lam_ref, q_ref, k_ref, v_ref, b_ref, g_ref, o_ref,
                 m_ref, l_ref, acc_ref, *, out_scale):
    qi = pl.program_id(2)
    tq = q_ref.shape[1]
    d_half = q_ref.shape[2] // 2

    q = q_ref[0] * (d_half ** -0.5)
    lane = lax.broadcasted_iota(jnp.int32, q.shape, 1)
    zero = jnp.zeros_like(q)
    qq = jnp.concatenate([jnp.where(lane < d_half, q, zero),
                          jnp.where(lane >= d_half, q, zero)], axis=0)

    m_ref[...] = jnp.full(m_ref.shape, NEG_INF, F32)
    l_ref[...] = jnp.zeros(l_ref.shape, F32)
    acc_ref[...] = jnp.zeros(acc_ref.shape, F32)

    def step(j, bias):
        start = pl.multiple_of(j * ATT_TK, ATT_TK)
        kj = k_ref[0, pl.ds(start, ATT_TK), :]
        vj = v_ref[0, pl.ds(start, ATT_TK), :]
        s = lax.dot_general(qq, kj, (((1,), (1,)), ((), ())), preferred_element_type=F32)
        if bias is not None:
            s = s + jnp.concatenate([bias, bias], axis=0)
        m_prev = m_ref[...]
        m_next = jnp.maximum(m_prev, jnp.max(s, axis=1, keepdims=True))
        alpha = jnp.exp(m_prev - m_next)
        p = jnp.exp(s - m_next[:, :1])
        l_ref[...] = alpha * l_ref[...] + jnp.sum(p, axis=1, keepdims=True)
        acc_ref[...] = alpha * acc_ref[...] + jnp.dot(p.astype(BF16), vj,
                                                      preferred_element_type=F32)
        m_ref[...] = m_next

    def far_body(j, carry):
        step(j, None)
        return carry

    lax.fori_loop(0, qi - 1, far_body, 0)

    @pl.when(qi >= 1)
    def _():
        step(qi - 1, b_ref[0, 1])

    step(qi, b_ref[0, 0])

    o = acc_ref[...] / l_ref[...]
    o = o[:tq] - lam_ref[0] * o[tq:]
    o_ref[0] = (_rms(o, g_ref[...]) * out_scale).astype(o_ref.dtype)


def _attention(za, bias, lam, g, lam_init):
    b, s, _ = za.shape
    dh = LANES
    kernel = functools.partial(_attn_kernel, out_scale=1.0 - lam_init)
    return pl.pallas_call(
        kernel,
        grid=(b, H_A, s // ATT_TQ),
        in_specs=[
            pl.BlockSpec(memory_space=pltpu.SMEM),
            pl.BlockSpec((1, ATT_TQ, dh), lambda bi, h, i: (bi, i, h)),
            pl.BlockSpec((1, s, dh), lambda bi, h, i: (bi, 0, H_A + h)),
            pl.BlockSpec((1, s, dh), lambda bi, h, i: (bi, 0, 2 * H_A + h)),
            pl.BlockSpec((1, 2, ATT_TQ, ATT_TK), lambda bi, h, i: (h, 0, 0, 0)),
            pl.BlockSpec((1, dh), lambda bi, h, i: (0, 0)),
        ],
        out_specs=pl.BlockSpec((1, ATT_TQ, dh), lambda bi, h, i: (bi, i, h)),
        out_shape=jax.ShapeDtypeStruct((b, s, H_A * dh), BF16),
        scratch_shapes=[
            pltpu.VMEM((2 * ATT_TQ, LANES), F32),
            pltpu.VMEM((2 * ATT_TQ, LANES), F32),
            pltpu.VMEM((2 * ATT_TQ, dh), F32),
        ],
        compiler_params=pltpu.CompilerParams(
            dimension_semantics=("parallel", "parallel", "arbitrary"),
            vmem_limit_bytes=VMEM_LIMIT),
        name="diff_attention",
    )(lam, za, za, za, bias, g)


def _rec_chunk(rq, rf, ri, log_lb, log_1mlb, st_ref, h):
    c = rq.shape[0]
    ls = jnp.minimum(rf, 0.0) - jnp.log1p(jnp.exp(-jnp.abs(rf)))
    bterm = log_1mlb + ls
    g = jnp.maximum(log_lb, bterm) + jnp.log1p(jnp.exp(-jnp.abs(log_lb - bterm)))
    kin = jnp.exp(bterm - rf)
    q = rq / (1.0 + jnp.exp(-rq))

    row = lax.broadcasted_iota(jnp.int32, (c, c), 0)
    col = lax.broadcasted_iota(jnp.int32, (c, c), 1)
    causal = col <= row
    bc = jnp.dot(causal.astype(F32), g, precision=lax.Precision.HIGHEST,
                 preferred_element_type=F32)
    blast = bc[c - 1:c]

    a_rows = []
    for i in range(c // SUB):
        lo, hi = i * SUB, (i + 1) * SUB
        mid = lo + SUB // 2
        u = jnp.clip(bc[lo:hi] - bc[mid:mid + 1], -EXP_CLAMP, EXP_CLAMP)
        qd = q[lo:hi] * jnp.exp(u)
        kd = kin[lo:hi] * jnp.exp(-u)
        kd_pad = jnp.concatenate(
            ([jnp.zeros((lo, LANES), F32)] if lo else []) + [kd]
            + ([jnp.zeros((c - hi, LANES), F32)] if c - hi else []), axis=0)
        if i == 0:
            lhs, rhs = qd, kd_pad
        else:
            ref = bc[lo:lo + 1]
            qo = q[lo:hi] * jnp.exp(bc[lo:hi] - ref)
            ko = kin[:lo] * jnp.exp(ref - bc[:lo])
            ko_pad = jnp.concatenate([ko, jnp.zeros((c - lo, LANES), F32)], axis=0)
            lhs = jnp.concatenate([qo, qd], axis=1)
            rhs = jnp.concatenate([ko_pad, kd_pad], axis=1)
        a_rows.append(lax.dot_general(lhs.astype(BF16), rhs.astype(BF16),
                                      (((1,), (1,)), ((), ())), preferred_element_type=F32))
    a = jnp.where(causal, jnp.concatenate(a_rows, axis=0), 0.0)
    v = ri.astype(BF16)
    intra = jnp.dot(a.astype(BF16), v, preferred_element_type=F32)

    st = st_ref[h]
    qe = (q * jnp.exp(bc)).astype(BF16)
    inter = lax.dot_general(qe, st.astype(BF16), (((1,), (1,)), ((), ())),
                            preferred_element_type=F32)
    ke = (kin * jnp.exp(blast - bc)).astype(BF16)
    st_ref[h] = st * jnp.exp(blast) + lax.dot_general(
        v, ke, (((0,), (0,)), ((), ())), preferred_element_type=F32)
    return inter + intra


def _rec_kernel(rq_ref, rf_ref, ri_ref, rg_ref, llb_ref, l1m_ref, g_ref, o_ref, st_ref):
    @pl.when(pl.program_id(1) == 0)
    def _():
        st_ref[...] = jnp.zeros(st_ref.shape, F32)

    ts = rq_ref.shape[1]

    def chunk_body(ci, carry):
        r0 = pl.multiple_of(ci * CHUNK, CHUNK)
        rows = pl.ds(r0, CHUNK)
        for h in range(H_R):
            cols = slice(h * LANES, (h + 1) * LANES)
            o = _rec_chunk(rq_ref[0, rows, cols], rf_ref[0, rows, cols], ri_ref[0, rows, cols],
                           llb_ref[:, cols], l1m_ref[:, cols], st_ref, h)
            rg = rg_ref[0, rows, cols]
            o = _rms(o, g_ref[...]) * (rg / (1.0 + jnp.exp(-rg)))
            o_ref[0, rows, cols] = o.astype(o_ref.dtype)
        return carry

    lax.fori_loop(0, ts // CHUNK, chunk_body, 0)


def _recurrence(zr, log_lb, log_1mlb, g):
    b, s, cols4 = zr.shape
    w = cols4 // 4
    spec = lambda j: pl.BlockSpec((1, REC_TS, w), lambda bi, si, j=j: (bi, si, j))
    vec = lambda n: pl.BlockSpec((1, n), lambda bi, si: (0, 0))
    return pl.pallas_call(
        _rec_kernel,
        grid=(b, s // REC_TS),
        in_specs=[spec(0), spec(1), spec(2), spec(3), vec(w), vec(w), vec(LANES)],
        out_specs=pl.BlockSpec((1, REC_TS, w), lambda bi, si: (bi, si, 0)),
        out_shape=jax.ShapeDtypeStruct((b, s, w), BF16),
        scratch_shapes=[pltpu.VMEM((H_R, LANES, LANES), F32)],
        compiler_params=pltpu.CompilerParams(
            dimension_semantics=("parallel", "arbitrary"), vmem_limit_bytes=VMEM_LIMIT),
        name="hgrn2",
    )(zr, zr, zr, zr, log_lb, log_1mlb, g)


def _mlp_kernel(x_ref, oa_ref, or_ref, wo_ref, g2_ref, wu_ref, wd_ref, gf_ref, o_ref, u2_ref,
                *, final_norm):
    wa = oa_ref.shape[1]
    x = x_ref[...]
    x = x + jnp.dot(oa_ref[...], wo_ref[:wa, :], preferred_element_type=F32)
    x = x + jnp.dot(or_ref[...], wo_ref[wa:, :], preferred_element_type=F32)
    h2 = _rms(x, g2_ref[...]).astype(BF16)
    d_ff = wu_ref.shape[1]
    for f in range(d_ff // COL_TILE):
        cols = slice(f * COL_TILE, (f + 1) * COL_TILE)
        u = jnp.maximum(jnp.dot(h2, wu_ref[:, cols], preferred_element_type=F32), 0.0)
        u2_ref[:, cols] = (u * u).astype(BF16)
    x = x + jnp.dot(u2_ref[...], wd_ref[...], preferred_element_type=F32)
    if final_norm:
        x = _rms(x, gf_ref[...])
    o_ref[...] = x


def _mlp(x2d, oa, orr, wo, g2, wu, wd, gf, final_norm):
    n, d = x2d.shape
    wa, wr = oa.shape[1], orr.shape[1]
    d_ff = wu.shape[1]
    const = lambda shape: pl.BlockSpec(shape, lambda i: (0, 0), pipeline_mode=pl.Buffered(1))
    return pl.pallas_call(
        functools.partial(_mlp_kernel, final_norm=final_norm),
        grid=(n // ROW_TILE,),
        in_specs=[
            pl.BlockSpec((ROW_TILE, d), lambda i: (i, 0)),
            pl.BlockSpec((ROW_TILE, wa), lambda i: (i, 0)),
            pl.BlockSpec((ROW_TILE, wr), lambda i: (i, 0)),
            const((wa + wr, d)),
            const((1, d)),
            const((d, d_ff)),
            const((d_ff, d)),
            const((1, d)),
        ],
        out_specs=pl.BlockSpec((ROW_TILE, d), lambda i: (i, 0)),
        out_shape=jax.ShapeDtypeStruct((n, d), F32),
        scratch_shapes=[pltpu.VMEM((ROW_TILE, d_ff), BF16)],
        compiler_params=pltpu.CompilerParams(
            dimension_semantics=("parallel",), vmem_limit_bytes=VMEM_LIMIT),
        name="outproj_mlp",
    )(x2d, oa, orr, wo, g2, wu, wd, gf)


def kernel(x, norm1_g, w_in, lam_qk, attn_norm_g, lb_logits, hgrn_norm_g, w_out, norm2_g,
           w_up, w_down, rel_bias, final_g):
    b, s, d = x.shape
    depth = w_in.shape[0]
    att_w = attn_norm_g.shape[1] * H_A
    rec_w = hgrn_norm_g.shape[1] * H_R
    assert attn_norm_g.shape[1] == LANES and hgrn_norm_g.shape[1] == LANES
    assert w_in.shape[2] == 3 * att_w + 4 * rec_w
    assert (b * s) % ROW_TILE == 0 and s % ATT_TQ == 0 and s % REC_TS == 0

    lb = jnp.cumsum(jax.nn.softmax(lb_logits.astype(F32), axis=0), axis=0)
    lb = lb - lb[0:1]
    log_lb = jnp.log(lb)
    log_1mlb = jnp.log1p(-lb)

    bias = _bias_tiles(rel_bias)
    x2d = x.reshape(b * s, d)
    for l in range(depth):
        za, zr = _inproj(x2d, norm1_g[l][None], w_in[l].astype(BF16), 3 * att_w)

        lam_init = 0.8 - 0.6 * math.exp(-0.3 * l)
        lq = lam_qk[l].astype(F32)
        lam = jnp.exp(jnp.sum(lq[0] * lq[1])) - jnp.exp(jnp.sum(lq[2] * lq[3])) + lam_init
        oa = _attention(za.reshape(b, s, 3 * att_w), bias, lam.reshape(1),
                        attn_norm_g[l][None].astype(F32), lam_init)
        orr = _recurrence(zr.reshape(b, s, 4 * rec_w), log_lb[l][None], log_1mlb[l][None],
                          hgrn_norm_g[l][None].astype(F32))
        x2d = _mlp(x2d, oa.reshape(b * s, att_w), orr.reshape(b * s, rec_w),
                   w_out[l].astype(BF16), norm2_g[l][None], w_up[l].astype(BF16),
                   w_down[l].astype(BF16), final_g[None], l == depth - 1)
    return x2d.reshape(b, s, d)
```

```python
import functools
import math

import jax
import jax.numpy as jnp
from jax import lax
from jax.experimental import pallas as pl
from jax.experimental.pallas import tpu as pltpu

CHUNK = 64
H_A = 4
H_R = 4
NUM_BUCKETS = 32
MAX_DISTANCE = 128
EPS = 1e-6
NEG_INF = -1e30

LANES = 128
ROW_TILE = 512
COL_TILE = 512
ATT_TQ = 256
ATT_TK = 256
REC_TS = 512
SUB = 16
EXP_CLAMP = 80.0
ONES_ROWS = 16
V_ROWS = LANES + ONES_ROWS
LOG2E = 1.4426950408889634
VMEM_LIMIT = 56 * 1024 * 1024

F32 = jnp.float32
BF16 = jnp.bfloat16


def _rms(x, g):
    return x * lax.rsqrt(jnp.mean(x * x, axis=-1, keepdims=True) + EPS) * g


def _inproj_kernel(x_ref, g_ref, wqt_ref, wk_ref, wvt_ref, wr_ref,
                   zqt_ref, zk_ref, zvt_ref, zr_ref, *, q_scale):
    h = _rms(x_ref[...], g_ref[...]).astype(BF16)
    nt = (((1,), (1,)), ((), ()))
    zqt_ref[...] = (lax.dot_general(wqt_ref[...], h, nt, preferred_element_type=F32)
                    * q_scale).astype(BF16)
    zk_ref[...] = jnp.dot(h, wk_ref[...], preferred_element_type=F32).astype(BF16)
    zvt = lax.dot_general(wvt_ref[...], h, nt, preferred_element_type=F32).astype(BF16)
    ones = jnp.ones((ONES_ROWS, ATT_TK), BF16)
    for c in range(zvt_ref.shape[0]):
        for hd in range(H_A):
            zvt_ref[c, hd * V_ROWS:hd * V_ROWS + LANES, :] = (
                zvt[hd * LANES:(hd + 1) * LANES, c * ATT_TK:(c + 1) * ATT_TK])
            zvt_ref[c, hd * V_ROWS + LANES:(hd + 1) * V_ROWS, :] = ones
    for j in range(wr_ref.shape[1] // COL_TILE):
        cols = slice(j * COL_TILE, (j + 1) * COL_TILE)
        zr_ref[:, cols] = jnp.dot(h, wr_ref[:, cols], preferred_element_type=F32)


def _inproj(x2d, g, wqt, wk, wvt, wr, q_scale):
    n, d = x2d.shape
    att_w = wk.shape[1]
    rec_cols = wr.shape[1]
    const = lambda shape: pl.BlockSpec(shape, lambda i: (0,) * len(shape),
                                       pipeline_mode=pl.Buffered(1))
    return pl.pallas_call(
        functools.partial(_inproj_kernel, q_scale=q_scale),
        grid=(n // ROW_TILE,),
        in_specs=[
            pl.BlockSpec((ROW_TILE, d), lambda i: (i, 0)),
            const((1, d)), const((att_w, d)), const((d, att_w)), const((att_w, d)),
            const((d, rec_cols)),
        ],
        out_specs=[
            pl.BlockSpec((att_w, ROW_TILE), lambda i: (0, i)),
            pl.BlockSpec((ROW_TILE, att_w), lambda i: (i, 0)),
            pl.BlockSpec((ROW_TILE // ATT_TK, H_A * V_ROWS, ATT_TK), lambda i: (i, 0, 0)),
            pl.BlockSpec((ROW_TILE, rec_cols), lambda i: (i, 0)),
        ],
        out_shape=[
            jax.ShapeDtypeStruct((att_w, n), BF16),
            jax.ShapeDtypeStruct((n, att_w), BF16),
            jax.ShapeDtypeStruct((n // ATT_TK, H_A * V_ROWS, ATT_TK), BF16),
            jax.ShapeDtypeStruct((n, rec_cols), F32),
        ],
        compiler_params=pltpu.CompilerParams(
            dimension_semantics=("parallel",), vmem_limit_bytes=VMEM_LIMIT),
        name="inproj",
    )(x2d, g, wqt, wk, wvt, wr)


def _t5_bucket(rel):
    n_half = NUM_BUCKETS // 2
    max_exact = n_half // 2
    ret = jnp.where(rel > 0, n_half, 0)
    n = jnp.abs(rel)
    nf = jnp.maximum(n, 1).astype(jnp.float32)
    large = max_exact + (jnp.log(nf / max_exact) / math.log(MAX_DISTANCE / max_exact)
                         * (n_half - max_exact)).astype(jnp.int32)
    large = jnp.minimum(large, n_half - 1)
    return ret + jnp.where(n < max_exact, n, large)


def _bias_kernel(rb_ref, idx_ref, o_ref):
    h = pl.program_id(0)
    far = rb_ref[NUM_BUCKETS // 2 - 1, h]
    for t in range(2):
        idx = idx_ref[t]
        acc = jnp.zeros(idx.shape, F32)
        for b in range(NUM_BUCKETS):
            acc = jnp.where(idx == b, (rb_ref[b, h] - far) * LOG2E, acc)
        if t == 0:
            krow = lax.broadcasted_iota(jnp.int32, idx.shape, 0)
            qcol = lax.broadcasted_iota(jnp.int32, idx.shape, 1)
            acc = jnp.where(krow // CHUNK <= qcol // CHUNK, acc, NEG_INF)
        o_ref[0, t] = jnp.concatenate([acc, acc], axis=1)


def _bias_tiles(rel_bias):
    assert ATT_TQ == ATT_TK and ATT_TQ >= MAX_DISTANCE and ATT_TQ % CHUNK == 0
    kpos = jnp.arange(ATT_TK)[:, None]
    qpos = jnp.arange(ATT_TQ)[None, :]
    idx = jnp.stack([_t5_bucket(kpos - qpos), _t5_bucket(kpos - ATT_TK - qpos)]).astype(jnp.int32)
    return pl.pallas_call(
        _bias_kernel,
        grid=(H_A,),
        in_specs=[
            pl.BlockSpec(memory_space=pltpu.SMEM),
            pl.BlockSpec((2, ATT_TK, ATT_TQ), lambda h: (0, 0, 0)),
        ],
        out_specs=pl.BlockSpec((1, 2, ATT_TK, 2 * ATT_TQ), lambda h: (h, 0, 0, 0)),
        out_shape=jax.ShapeDtypeStruct((H_A, 2, ATT_TK, 2 * ATT_TQ), F32),
        name="bias_tiles",
    )(rel_bias.astype(F32), idx)


def _attn_kernel(lam_ref, qt_ref, k_ref, vt_ref, b_ref, g_ref, o_ref,
                 qq_ref, m_ref, acc_ref, *, out_scale):
    qi = pl.program_id(1)
    tq = qt_ref.shape[1]
    half = LANES // 2

    zero = jnp.zeros((half, tq), BF16)
    for h in range(H_A):
        qt = qt_ref[h * LANES:(h + 1) * LANES, :]
        qq_ref[h] = jnp.concatenate(
            [jnp.concatenate([qt[:half], zero], axis=0),
             jnp.concatenate([zero, qt[half:]], axis=0)], axis=1)
    m_ref[...] = jnp.full(m_ref.shape, NEG_INF, F32)
    acc_ref[...] = jnp.zeros(acc_ref.shape, F32)

    def step(j, bias_tile):
        start = pl.multiple_of(j * ATT_TK, ATT_TK)
        for h in range(H_A):
            kj = k_ref[0, pl.ds(start, ATT_TK), h * LANES:(h + 1) * LANES]
            s = jnp.dot(kj, qq_ref[h], preferred_element_type=F32)
            if bias_tile is not None:
                s = s + b_ref[h, bias_tile]
            m_prev = m_ref[h]
            m_next = jnp.maximum(m_prev, jnp.max(s, axis=0, keepdims=True))
            p = jnp.exp2(s - m_next).astype(BF16)
            alpha = jnp.exp2(m_prev - m_next)
            vj = vt_ref[j, h * V_ROWS:(h + 1) * V_ROWS, :]
            acc_ref[h] = alpha * acc_ref[h] + jnp.dot(vj, p, preferred_element_type=F32)
            m_ref[h] = m_next

    def far_body(j, carry):
        step(j, None)
        return carry

    lax.fori_loop(0, qi - 1, far_body, 0)

    @pl.when(qi >= 1)
    def _():
        step(qi - 1, 1)

    step(qi, 0)

    for h in range(H_A):
        acc = acc_ref[h]
        ot = acc[:LANES] / acc[LANES:LANES + 1]
        o = jnp.transpose(ot[:, :tq] - lam_ref[0] * ot[:, tq:])
        o_ref[0, :, h * LANES:(h + 1) * LANES] = (
            _rms(o, g_ref[...]) * out_scale).astype(o_ref.dtype)


def _attention(zqt, zk, zvt, bias, lam, g, lam_init, b, s):
    att_w = zk.shape[1]
    nq = s // ATT_TQ
    nk = s // ATT_TK
    kernel = functools.partial(_attn_kernel, out_scale=1.0 - lam_init)
    return pl.pallas_call(
        kernel,
        grid=(b, nq),
        in_specs=[
            pl.BlockSpec(memory_space=pltpu.SMEM),
            pl.BlockSpec((att_w, ATT_TQ), lambda bi, i: (0, bi * nq + i)),
            pl.BlockSpec((1, s, att_w), lambda bi, i: (bi, 0, 0)),
            pl.BlockSpec((nk, H_A * V_ROWS, ATT_TK), lambda bi, i: (bi, 0, 0)),
            pl.BlockSpec((H_A, 2, ATT_TK, 2 * ATT_TQ), lambda bi, i: (0, 0, 0, 0),
                         pipeline_mode=pl.Buffered(1)),
            pl.BlockSpec((1, LANES), lambda bi, i: (0, 0)),
        ],
        out_specs=pl.BlockSpec((1, ATT_TQ, att_w), lambda bi, i: (bi, i, 0)),
        out_shape=jax.ShapeDtypeStruct((b, s, att_w), BF16),
        scratch_shapes=[
            pltpu.VMEM((H_A, LANES, 2 * ATT_TQ), BF16),
            pltpu.VMEM((H_A, 1, 2 * ATT_TQ), F32),
            pltpu.VMEM((H_A, V_ROWS, 2 * ATT_TQ), F32),
        ],
        compiler_params=pltpu.CompilerParams(
            dimension_semantics=("parallel", "arbitrary"), vmem_limit_bytes=VMEM_LIMIT),
        name="diff_attention",
    )(lam, zqt, zk.reshape(b, s, att_w), zvt, bias, g)


def _rec_chunk(rq, rf, ri, log_lb, log_1mlb, st_ref, h):
    c = rq.shape[0]
    ls = jnp.minimum(rf, 0.0) - jnp.log1p(jnp.exp(-jnp.abs(rf)))
    bterm = log_1mlb + ls
    g = jnp.maximum(log_lb, bterm) + jnp.log1p(jnp.exp(-jnp.abs(log_lb - bterm)))
    kin = jnp.exp(bterm - rf)
    q = rq / (1.0 + jnp.exp(-rq))

    row = lax.broadcasted_iota(jnp.int32, (c, c), 0)
    col = lax.broadcasted_iota(jnp.int32, (c, c), 1)
    causal = col <= row
    bc = jnp.dot(causal.astype(F32), g, precision=lax.Precision.HIGHEST,
                 preferred_element_type=F32)
    blast = bc[c - 1:c]

    a_rows = []
    for i in range(c // SUB):
        lo, hi = i * SUB, (i + 1) * SUB
        mid = lo + SUB // 2
        u = jnp.clip(bc[lo:hi] - bc[mid:mid + 1], -EXP_CLAMP, EXP_CLAMP)
        qd = q[lo:hi] * jnp.exp(u)
        kd = kin[lo:hi] * jnp.exp(-u)
        kd_pad = jnp.concatenate(
            ([jnp.zeros((lo, LANES), F32)] if lo else []) + [kd]
            + ([jnp.zeros((c - hi, LANES), F32)] if c - hi else []), axis=0)
        if i == 0:
            lhs, rhs = qd, kd_pad
        else:
            ref = bc[lo:lo + 1]
            qo = q[lo:hi] * jnp.exp(bc[lo:hi] - ref)
            ko = kin[:lo] * jnp.exp(ref - bc[:lo])
            ko_pad = jnp.concatenate([ko, jnp.zeros((c - lo, LANES), F32)], axis=0)
            lhs = jnp.concatenate([qo, qd], axis=1)
            rhs = jnp.concatenate([ko_pad, kd_pad], axis=1)
        a_rows.append(lax.dot_general(lhs.astype(BF16), rhs.astype(BF16),
                                      (((1,), (1,)), ((), ())), preferred_element_type=F32))
    a = jnp.where(causal, jnp.concatenate(a_rows, axis=0), 0.0)
    v = ri.astype(BF16)
    intra = jnp.dot(a.astype(BF16), v, preferred_element_type=F32)

    st = st_ref[h]
    qe = (q * jnp.exp(bc)).astype(BF16)
    inter = lax.dot_general(qe, st.astype(BF16), (((1,), (1,)), ((), ())),
                            preferred_element_type=F32)
    ke = (kin * jnp.exp(blast - bc)).astype(BF16)
    st_ref[h] = st * jnp.exp(blast) + lax.dot_general(
        v, ke, (((0,), (0,)), ((), ())), preferred_element_type=F32)
    return inter + intra


def _rec_kernel(rq_ref, rf_ref, ri_ref, rg_ref, llb_ref, l1m_ref, g_ref, o_ref, st_ref):
    @pl.when(pl.program_id(1) == 0)
    def _():
        st_ref[...] = jnp.zeros(st_ref.shape, F32)

    ts = rq_ref.shape[1]

    def chunk_body(ci, carry):
        r0 = pl.multiple_of(ci * CHUNK, CHUNK)
        rows = pl.ds(r0, CHUNK)
        for h in range(H_R):
            cols = slice(h * LANES, (h + 1) * LANES)
            o = _rec_chunk(rq_ref[0, rows, cols], rf_ref[0, rows, cols], ri_ref[0, rows, cols],
                           llb_ref[:, cols], l1m_ref[:, cols], st_ref, h)
            rg = rg_ref[0, rows, cols]
            o = _rms(o, g_ref[...]) * (rg / (1.0 + jnp.exp(-rg)))
            o_ref[0, rows, cols] = o.astype(o_ref.dtype)
        return carry

    lax.fori_loop(0, ts // CHUNK, chunk_body, 0)


def _recurrence(zr, log_lb, log_1mlb, g):
    b, s, cols4 = zr.shape
    w = cols4 // 4
    spec = lambda j: pl.BlockSpec((1, REC_TS, w), lambda bi, si, j=j: (bi, si, j))
    vec = lambda n: pl.BlockSpec((1, n), lambda bi, si: (0, 0))
    return pl.pallas_call(
        _rec_kernel,
        grid=(b, s // REC_TS),
        in_specs=[spec(0), spec(1), spec(2), spec(3), vec(w), vec(w), vec(LANES)],
        out_specs=pl.BlockSpec((1, REC_TS, w), lambda bi, si: (bi, si, 0)),
        out_shape=jax.ShapeDtypeStruct((b, s, w), BF16),
        scratch_shapes=[pltpu.VMEM((H_R, LANES, LANES), F32)],
        compiler_params=pltpu.CompilerParams(
            dimension_semantics=("parallel", "arbitrary"), vmem_limit_bytes=VMEM_LIMIT),
        name="hgrn2",
    )(zr, zr, zr, zr, log_lb, log_1mlb, g)


def _mlp_kernel(x_ref, oa_ref, or_ref, wo_ref, g2_ref, wu_ref, wd_ref, gf_ref, o_ref, u2_ref,
                *, final_norm):
    wa = oa_ref.shape[1]
    x = x_ref[...]
    x = x + jnp.dot(oa_ref[...], wo_ref[:wa, :], preferred_element_type=F32)
    x = x + jnp.dot(or_ref[...], wo_ref[wa:, :], preferred_element_type=F32)
    h2 = _rms(x, g2_ref[...]).astype(BF16)
    d_ff = wu_ref.shape[1]
    for f in range(d_ff // COL_TILE):
        cols = slice(f * COL_TILE, (f + 1) * COL_TILE)
        u = jnp.maximum(jnp.dot(h2, wu_ref[:, cols], preferred_element_type=F32), 0.0)
        u2_ref[:, cols] = (u * u).astype(BF16)
    x = x + jnp.dot(u2_ref[...], wd_ref[...], preferred_element_type=F32)
    if final_norm:
        x = _rms(x, gf_ref[...])
    o_ref[...] = x


def _mlp(x2d, oa, orr, wo, g2, wu, wd, gf, final_norm):
    n, d = x2d.shape
    wa, wr = oa.shape[1], orr.shape[1]
    d_ff = wu.shape[1]
    const = lambda shape: pl.BlockSpec(shape, lambda i: (0, 0), pipeline_mode=pl.Buffered(1))
    return pl.pallas_call(
        functools.partial(_mlp_kernel, final_norm=final_norm),
        grid=(n // ROW_TILE,),
        in_specs=[
            pl.BlockSpec((ROW_TILE, d), lambda i: (i, 0)),
            pl.BlockSpec((ROW_TILE, wa), lambda i: (i, 0)),
            pl.BlockSpec((ROW_TILE, wr), lambda i: (i, 0)),
            const((wa + wr, d)),
            const((1, d)),
            const((d, d_ff)),
            const((d_ff, d)),
            const((1, d)),
        ],
        out_specs=pl.BlockSpec((ROW_TILE, d), lambda i: (i, 0)),
        out_shape=jax.ShapeDtypeStruct((n, d), F32),
        scratch_shapes=[pltpu.VMEM((ROW_TILE, d_ff), BF16)],
        compiler_params=pltpu.CompilerParams(
            dimension_semantics=("parallel",), vmem_limit_bytes=VMEM_LIMIT),
        name="outproj_mlp",
    )(x2d, oa, orr, wo, g2, wu, wd, gf)


def kernel(x, norm1_g, w_in, lam_qk, attn_norm_g, lb_logits, hgrn_norm_g, w_out, norm2_g,
           w_up, w_down, rel_bias, final_g):
    b, s, d = x.shape
    depth = w_in.shape[0]
    att_w = attn_norm_g.shape[1] * H_A
    rec_w = hgrn_norm_g.shape[1] * H_R
    assert attn_norm_g.shape[1] == LANES and hgrn_norm_g.shape[1] == LANES
    assert w_in.shape[2] == 3 * att_w + 4 * rec_w
    assert (b * s) % ROW_TILE == 0 and s % ATT_TQ == 0 and s % REC_TS == 0

    lb = jnp.cumsum(jax.nn.softmax(lb_logits.astype(F32), axis=0), axis=0)
    lb = lb - lb[0:1]
    log_lb = jnp.log(lb)
    log_1mlb = jnp.log1p(-lb)

    bias = _bias_tiles(rel_bias)
    x2d = x.reshape(b * s, d)
    q_scale = (att_w // (2 * H_A)) ** -0.5 * LOG2E
    for l in range(depth):
        w = w_in[l].astype(BF16)
        zqt, zk, zvt, zr = _inproj(
            x2d, norm1_g[l][None], w[:, :att_w].T, w[:, att_w:2 * att_w],
            w[:, 2 * att_w:3 * att_w].T, w[:, 3 * att_w:], q_scale)

        lam_init = 0.8 - 0.6 * math.exp(-0.3 * l)
        lq = lam_qk[l].astype(F32)
        lam = jnp.exp(jnp.sum(lq[0] * lq[1])) - jnp.exp(jnp.sum(lq[2] * lq[3])) + lam_init
        oa = _attention(zqt, zk, zvt, bias, lam.reshape(1), attn_norm_g[l][None].astype(F32),
                        lam_init, b, s)
        orr = _recurrence(zr.reshape(b, s, 4 * rec_w), log_lb[l][None], log_1mlb[l][None],
                          hgrn_norm_g[l][None].astype(F32))
        x2d = _mlp(x2d, oa.reshape(b * s, att_w), orr.reshape(b * s, rec_w),
                   w_out[l].astype(BF16), norm2_g[l][None], w_up[l].astype(BF16),
                   w_down[l].astype(BF16), final_g[None], l == depth - 1)
    return x2d.reshape(b, s, d)
```

```python
import functools
import math

import jax
import jax.numpy as jnp
from jax import lax
from jax.experimental import pallas as pl
from jax.experimental.pallas import tpu as pltpu

CHUNK = 64
H_A = 4
H_R = 4
NUM_BUCKETS = 32
MAX_DISTANCE = 128
EPS = 1e-6
NEG_INF = -1e30

LANES = 128
ROW_TILE = 512
COL_TILE = 512
ATT_TQ = 256
ATT_TK = 256
QK_AHEAD = 2
REC_TS = 512
SUB = 16
EXP_CLAMP = 80.0
ONES_ROWS = 16
V_ROWS = LANES + ONES_ROWS
LOG2E = 1.4426950408889634
VMEM_LIMIT = 56 * 1024 * 1024

F32 = jnp.float32
BF16 = jnp.bfloat16


def _rms(x, g):
    return x * lax.rsqrt(jnp.mean(x * x, axis=-1, keepdims=True) + EPS) * g


def _inproj_kernel(x_ref, g_ref, wqt_ref, wk_ref, wvt_ref, wr_ref,
                   zqt_ref, zk_ref, zvt_ref, zr_ref, *, q_scale):
    h = _rms(x_ref[...], g_ref[...]).astype(BF16)
    nt = (((1,), (1,)), ((), ()))
    zqt_ref[...] = (lax.dot_general(wqt_ref[...], h, nt, preferred_element_type=F32)
                    * q_scale).astype(BF16)
    zk_ref[...] = jnp.dot(h, wk_ref[...], preferred_element_type=F32).astype(BF16)
    zvt = lax.dot_general(wvt_ref[...], h, nt, preferred_element_type=F32).astype(BF16)
    ones = jnp.ones((ONES_ROWS, ATT_TK), BF16)
    for c in range(zvt_ref.shape[0]):
        for hd in range(H_A):
            zvt_ref[c, hd * V_ROWS:hd * V_ROWS + LANES, :] = (
                zvt[hd * LANES:(hd + 1) * LANES, c * ATT_TK:(c + 1) * ATT_TK])
            zvt_ref[c, hd * V_ROWS + LANES:(hd + 1) * V_ROWS, :] = ones
    for j in range(wr_ref.shape[1] // COL_TILE):
        cols = slice(j * COL_TILE, (j + 1) * COL_TILE)
        zr_ref[:, cols] = jnp.dot(h, wr_ref[:, cols], preferred_element_type=F32)


def _inproj(x2d, g, wqt, wk, wvt, wr, q_scale):
    n, d = x2d.shape
    att_w = wk.shape[1]
    rec_cols = wr.shape[1]
    const = lambda shape: pl.BlockSpec(shape, lambda i: (0,) * len(shape),
                                       pipeline_mode=pl.Buffered(1))
    return pl.pallas_call(
        functools.partial(_inproj_kernel, q_scale=q_scale),
        grid=(n // ROW_TILE,),
        in_specs=[
            pl.BlockSpec((ROW_TILE, d), lambda i: (i, 0)),
            const((1, d)), const((att_w, d)), const((d, att_w)), const((att_w, d)),
            const((d, rec_cols)),
        ],
        out_specs=[
            pl.BlockSpec((att_w, ROW_TILE), lambda i: (0, i)),
            pl.BlockSpec((ROW_TILE, att_w), lambda i: (i, 0)),
            pl.BlockSpec((ROW_TILE // ATT_TK, H_A * V_ROWS, ATT_TK), lambda i: (i, 0, 0)),
            pl.BlockSpec((ROW_TILE, rec_cols), lambda i: (i, 0)),
        ],
        out_shape=[
            jax.ShapeDtypeStruct((att_w, n), BF16),
            jax.ShapeDtypeStruct((n, att_w), BF16),
            jax.ShapeDtypeStruct((n // ATT_TK, H_A * V_ROWS, ATT_TK), BF16),
            jax.ShapeDtypeStruct((n, rec_cols), F32),
        ],
        compiler_params=pltpu.CompilerParams(
            dimension_semantics=("parallel",), vmem_limit_bytes=VMEM_LIMIT),
        name="inproj",
    )(x2d, g, wqt, wk, wvt, wr)


def _t5_bucket(rel):
    n_half = NUM_BUCKETS // 2
    max_exact = n_half // 2
    ret = jnp.where(rel > 0, n_half, 0)
    n = jnp.abs(rel)
    nf = jnp.maximum(n, 1).astype(jnp.float32)
    large = max_exact + (jnp.log(nf / max_exact) / math.log(MAX_DISTANCE / max_exact)
                         * (n_half - max_exact)).astype(jnp.int32)
    large = jnp.minimum(large, n_half - 1)
    return ret + jnp.where(n < max_exact, n, large)


def _bias_kernel(rb_ref, idx_ref, o_ref):
    h = pl.program_id(0)
    far = rb_ref[NUM_BUCKETS // 2 - 1, h]
    for t in range(2):
        idx = idx_ref[t]
        acc = jnp.zeros(idx.shape, F32)
        for b in range(NUM_BUCKETS):
            acc = jnp.where(idx == b, (rb_ref[b, h] - far) * LOG2E, acc)
        if t == 0:
            krow = lax.broadcasted_iota(jnp.int32, idx.shape, 0)
            qcol = lax.broadcasted_iota(jnp.int32, idx.shape, 1)
            acc = jnp.where(krow // CHUNK <= qcol // CHUNK, acc, NEG_INF)
        o_ref[0, t] = jnp.concatenate([acc, acc], axis=1)


def _bias_tiles(rel_bias):
    assert ATT_TQ == ATT_TK and ATT_TQ >= MAX_DISTANCE and ATT_TQ % CHUNK == 0
    kpos = jnp.arange(ATT_TK)[:, None]
    qpos = jnp.arange(ATT_TQ)[None, :]
    idx = jnp.stack([_t5_bucket(kpos - qpos), _t5_bucket(kpos - ATT_TK - qpos)]).astype(jnp.int32)
    return pl.pallas_call(
        _bias_kernel,
        grid=(H_A,),
        in_specs=[
            pl.BlockSpec(memory_space=pltpu.SMEM),
            pl.BlockSpec((2, ATT_TK, ATT_TQ), lambda h: (0, 0, 0)),
        ],
        out_specs=pl.BlockSpec((1, 2, ATT_TK, 2 * ATT_TQ), lambda h: (h, 0, 0, 0)),
        out_shape=jax.ShapeDtypeStruct((H_A, 2, ATT_TK, 2 * ATT_TQ), F32),
        name="bias_tiles",
    )(rel_bias.astype(F32), idx)


def _attn_kernel(lam_ref, qt_ref, k_ref, vt_ref, b_ref, g_ref, o_ref,
                 qq_ref, m_ref, acc_ref, *, out_scale):
    qi = pl.program_id(1)
    tq = qt_ref.shape[1]
    half = LANES // 2

    zero = jnp.zeros((half, tq), BF16)
    for h in range(H_A):
        qt = qt_ref[h * LANES:(h + 1) * LANES, :]
        qq_ref[h] = jnp.concatenate(
            [jnp.concatenate([qt[:half], zero], axis=0),
             jnp.concatenate([zero, qt[half:]], axis=0)], axis=1)
    m_ref[...] = jnp.full(m_ref.shape, NEG_INF, F32)
    acc_ref[...] = jnp.zeros(acc_ref.shape, F32)

    def scores(j, h):
        start = pl.multiple_of(j * ATT_TK, ATT_TK)
        kj = k_ref[0, pl.ds(start, ATT_TK), h * LANES:(h + 1) * LANES]
        return jnp.dot(kj, qq_ref[h], preferred_element_type=F32)

    def update(j, h, s, bias_tile):
        if bias_tile is not None:
            s = s + b_ref[h, bias_tile]
        m_prev = m_ref[h]
        m_next = jnp.maximum(m_prev, jnp.max(s, axis=0, keepdims=True))
        p = jnp.exp2(s - m_next).astype(BF16)
        alpha = jnp.exp2(m_prev - m_next)
        vj = vt_ref[j, h * V_ROWS:(h + 1) * V_ROWS, :]
        acc_ref[h] = alpha * acc_ref[h] + jnp.dot(vj, p, preferred_element_type=F32)
        m_ref[h] = m_next

    def run(blocks):
        items = [(j, h, bt) for j, bt in blocks for h in range(H_A)]
        pending = [scores(j, h) for j, h, _ in items[:QK_AHEAD]]
        for n, (j, h, bt) in enumerate(items):
            if n + QK_AHEAD < len(items):
                jn, hn, _ = items[n + QK_AHEAD]
                pending.append(scores(jn, hn))
            update(j, h, pending.pop(0), bt)

    n_far = jnp.maximum(qi - 1, 0)

    def far_pair(i, carry):
        run([(2 * i, None), (2 * i + 1, None)])
        return carry

    lax.fori_loop(0, n_far // 2, far_pair, 0)

    @pl.when(n_far % 2 == 1)
    def _():
        run([(n_far - 1, None)])

    @pl.when(qi >= 1)
    def _():
        run([(qi - 1, 1), (qi, 0)])

    @pl.when(qi == 0)
    def _():
        run([(qi, 0)])

    for h in range(H_A):
        acc = acc_ref[h]
        ot = acc[:LANES] / acc[LANES:LANES + 1]
        o = jnp.transpose(ot[:, :tq] - lam_ref[0] * ot[:, tq:])
        o_ref[0, :, h * LANES:(h + 1) * LANES] = (
            _rms(o, g_ref[...]) * out_scale).astype(o_ref.dtype)


def _attention(zqt, zk, zvt, bias, lam, g, lam_init, b, s):
    att_w = zk.shape[1]
    nq = s // ATT_TQ
    nk = s // ATT_TK
    kernel = functools.partial(_attn_kernel, out_scale=1.0 - lam_init)
    return pl.pallas_call(
        kernel,
        grid=(b, nq),
        in_specs=[
            pl.BlockSpec(memory_space=pltpu.SMEM),
            pl.BlockSpec((att_w, ATT_TQ), lambda bi, i: (0, bi * nq + i)),
            pl.BlockSpec((1, s, att_w), lambda bi, i: (bi, 0, 0)),
            pl.BlockSpec((nk, H_A * V_ROWS, ATT_TK), lambda bi, i: (bi, 0, 0)),
            pl.BlockSpec((H_A, 2, ATT_TK, 2 * ATT_TQ), lambda bi, i: (0, 0, 0, 0),
                         pipeline_mode=pl.Buffered(1)),
            pl.BlockSpec((1, LANES), lambda bi, i: (0, 0)),
        ],
        out_specs=pl.BlockSpec((1, ATT_TQ, att_w), lambda bi, i: (bi, i, 0)),
        out_shape=jax.ShapeDtypeStruct((b, s, att_w), BF16),
        scratch_shapes=[
            pltpu.VMEM((H_A, LANES, 2 * ATT_TQ), BF16),
            pltpu.VMEM((H_A, 1, 2 * ATT_TQ), F32),
            pltpu.VMEM((H_A, V_ROWS, 2 * ATT_TQ), F32),
        ],
        compiler_params=pltpu.CompilerParams(
            dimension_semantics=("parallel", "arbitrary"), vmem_limit_bytes=VMEM_LIMIT),
        name="diff_attention",
    )(lam, zqt, zk.reshape(b, s, att_w), zvt, bias, g)


def _rec_chunk(rq, rf, ri, log_lb, log_1mlb, st_ref, h):
    c = rq.shape[0]
    ls = jnp.minimum(rf, 0.0) - jnp.log1p(jnp.exp(-jnp.abs(rf)))
    bterm = log_1mlb + ls
    g = jnp.maximum(log_lb, bterm) + jnp.log1p(jnp.exp(-jnp.abs(log_lb - bterm)))
    kin = jnp.exp(bterm - rf)
    q = rq / (1.0 + jnp.exp(-rq))

    row = lax.broadcasted_iota(jnp.int32, (c, c), 0)
    col = lax.broadcasted_iota(jnp.int32, (c, c), 1)
    causal = col <= row
    bc = jnp.dot(causal.astype(F32), g, precision=lax.Precision.HIGHEST,
                 preferred_element_type=F32)
    blast = bc[c - 1:c]

    a_rows = []
    for i in range(c // SUB):
        lo, hi = i * SUB, (i + 1) * SUB
        mid = lo + SUB // 2
        u = jnp.clip(bc[lo:hi] - bc[mid:mid + 1], -EXP_CLAMP, EXP_CLAMP)
        qd = q[lo:hi] * jnp.exp(u)
        kd = kin[lo:hi] * jnp.exp(-u)
        kd_pad = jnp.concatenate(
            ([jnp.zeros((lo, LANES), F32)] if lo else []) + [kd]
            + ([jnp.zeros((c - hi, LANES), F32)] if c - hi else []), axis=0)
        if i == 0:
            lhs, rhs = qd, kd_pad
        else:
            ref = bc[lo:lo + 1]
            qo = q[lo:hi] * jnp.exp(bc[lo:hi] - ref)
            ko = kin[:lo] * jnp.exp(ref - bc[:lo])
            ko_pad = jnp.concatenate([ko, jnp.zeros((c - lo, LANES), F32)], axis=0)
            lhs = jnp.concatenate([qo, qd], axis=1)
            rhs = jnp.concatenate([ko_pad, kd_pad], axis=1)
        a_rows.append(lax.dot_general(lhs.astype(BF16), rhs.astype(BF16),
                                      (((1,), (1,)), ((), ())), preferred_element_type=F32))
    a = jnp.where(causal, jnp.concatenate(a_rows, axis=0), 0.0)
    v = ri.astype(BF16)
    intra = jnp.dot(a.astype(BF16), v, preferred_element_type=F32)

    st = st_ref[h]
    qe = (q * jnp.exp(bc)).astype(BF16)
    inter = lax.dot_general(qe, st.astype(BF16), (((1,), (1,)), ((), ())),
                            preferred_element_type=F32)
    ke = (kin * jnp.exp(blast - bc)).astype(BF16)
    st_ref[h] = st * jnp.exp(blast) + lax.dot_general(
        v, ke, (((0,), (0,)), ((), ())), preferred_element_type=F32)
    return inter + intra


def _rec_kernel(rq_ref, rf_ref, ri_ref, rg_ref, llb_ref, l1m_ref, g_ref, o_ref, st_ref):
    @pl.when(pl.program_id(1) == 0)
    def _():
        st_ref[...] = jnp.zeros(st_ref.shape, F32)

    ts = rq_ref.shape[1]

    def chunk_body(ci, carry):
        r0 = pl.multiple_of(ci * CHUNK, CHUNK)
        rows = pl.ds(r0, CHUNK)
        for h in range(H_R):
            cols = slice(h * LANES, (h + 1) * LANES)
            o = _rec_chunk(rq_ref[0, rows, cols], rf_ref[0, rows, cols], ri_ref[0, rows, cols],
                           llb_ref[:, cols], l1m_ref[:, cols], st_ref, h)
            rg = rg_ref[0, rows, cols]
            o = _rms(o, g_ref[...]) * (rg / (1.0 + jnp.exp(-rg)))
            o_ref[0, rows, cols] = o.astype(o_ref.dtype)
        return carry

    lax.fori_loop(0, ts // CHUNK, chunk_body, 0)


def _recurrence(zr, log_lb, log_1mlb, g):
    b, s, cols4 = zr.shape
    w = cols4 // 4
    spec = lambda j: pl.BlockSpec((1, REC_TS, w), lambda bi, si, j=j: (bi, si, j))
    vec = lambda n: pl.BlockSpec((1, n), lambda bi, si: (0, 0))
    return pl.pallas_call(
        _rec_kernel,
        grid=(b, s // REC_TS),
        in_specs=[spec(0), spec(1), spec(2), spec(3), vec(w), vec(w), vec(LANES)],
        out_specs=pl.BlockSpec((1, REC_TS, w), lambda bi, si: (bi, si, 0)),
        out_shape=jax.ShapeDtypeStruct((b, s, w), BF16),
        scratch_shapes=[pltpu.VMEM((H_R, LANES, LANES), F32)],
        compiler_params=pltpu.CompilerParams(
            dimension_semantics=("parallel", "arbitrary"), vmem_limit_bytes=VMEM_LIMIT),
        name="hgrn2",
    )(zr, zr, zr, zr, log_lb, log_1mlb, g)


def _mlp_kernel(x_ref, oa_ref, or_ref, wo_ref, g2_ref, wu_ref, wd_ref, gf_ref, o_ref, u2_ref,
                *, final_norm):
    wa = oa_ref.shape[1]
    x = x_ref[...]
    x = x + jnp.dot(oa_ref[...], wo_ref[:wa, :], preferred_element_type=F32)
    x = x + jnp.dot(or_ref[...], wo_ref[wa:, :], preferred_element_type=F32)
    h2 = _rms(x, g2_ref[...]).astype(BF16)
    d_ff = wu_ref.shape[1]
    for f in range(d_ff // COL_TILE):
        cols = slice(f * COL_TILE, (f + 1) * COL_TILE)
        u = jnp.maximum(jnp.dot(h2, wu_ref[:, cols], preferred_element_type=F32), 0.0)
        u2_ref[:, cols] = (u * u).astype(BF16)
    x = x + jnp.dot(u2_ref[...], wd_ref[...], preferred_element_type=F32)
    if final_norm:
        x = _rms(x, gf_ref[...])
    o_ref[...] = x


def _mlp(x2d, oa, orr, wo, g2, wu, wd, gf, final_norm):
    n, d = x2d.shape
    wa, wr = oa.shape[1], orr.shape[1]
    d_ff = wu.shape[1]
    const = lambda shape: pl.BlockSpec(shape, lambda i: (0, 0), pipeline_mode=pl.Buffered(1))
    return pl.pallas_call(
        functools.partial(_mlp_kernel, final_norm=final_norm),
        grid=(n // ROW_TILE,),
        in_specs=[
            pl.BlockSpec((ROW_TILE, d), lambda i: (i, 0)),
            pl.BlockSpec((ROW_TILE, wa), lambda i: (i, 0)),
            pl.BlockSpec((ROW_TILE, wr), lambda i: (i, 0)),
            const((wa + wr, d)),
            const((1, d)),
            const((d, d_ff)),
            const((d_ff, d)),
            const((1, d)),
        ],
        out_specs=pl.BlockSpec((ROW_TILE, d), lambda i: (i, 0)),
        out_shape=jax.ShapeDtypeStruct((n, d), F32),
        scratch_shapes=[pltpu.VMEM((ROW_TILE, d_ff), BF16)],
        compiler_params=pltpu.CompilerParams(
            dimension_semantics=("parallel",), vmem_limit_bytes=VMEM_LIMIT),
        name="outproj_mlp",
    )(x2d, oa, orr, wo, g2, wu, wd, gf)


def kernel(x, norm1_g, w_in, lam_qk, attn_norm_g, lb_logits, hgrn_norm_g, w_out, norm2_g,
           w_up, w_down, rel_bias, final_g):
    b, s, d = x.shape
    depth = w_in.shape[0]
    att_w = attn_norm_g.shape[1] * H_A
    rec_w = hgrn_norm_g.shape[1] * H_R
    assert attn_norm_g.shape[1] == LANES and hgrn_norm_g.shape[1] == LANES
    assert w_in.shape[2] == 3 * att_w + 4 * rec_w
    assert (b * s) % ROW_TILE == 0 and s % ATT_TQ == 0 and s % REC_TS == 0

    lb = jnp.cumsum(jax.nn.softmax(lb_logits.astype(F32), axis=0), axis=0)
    lb = lb - lb[0:1]
    log_lb = jnp.log(lb)
    log_1mlb = jnp.log1p(-lb)

    bias = _bias_tiles(rel_bias)
    x2d = x.reshape(b * s, d)
    q_scale = (att_w // (2 * H_A)) ** -0.5 * LOG2E
    for l in range(depth):
        w = w_in[l].astype(BF16)
        zqt, zk, zvt, zr = _inproj(
            x2d, norm1_g[l][None], w[:, :att_w].T, w[:, att_w:2 * att_w],
            w[:, 2 * att_w:3 * att_w].T, w[:, 3 * att_w:], q_scale)

        lam_init = 0.8 - 0.6 * math.exp(-0.3 * l)
        lq = lam_qk[l].astype(F32)
        lam = jnp.exp(jnp.sum(lq[0] * lq[1])) - jnp.exp(jnp.sum(lq[2] * lq[3])) + lam_init
        oa = _attention(zqt, zk, zvt, bias, lam.reshape(1), attn_norm_g[l][None].astype(F32),
                        lam_init, b, s)
        orr = _recurrence(zr.reshape(b, s, 4 * rec_w), log_lb[l][None], log_1mlb[l][None],
                          hgrn_norm_g[l][None].astype(F32))
        x2d = _mlp(x2d, oa.reshape(b * s, att_w), orr.reshape(b * s, rec_w),
                   w_out[l].astype(BF16), norm2_g[l][None], w_up[l].astype(BF16),
                   w_down[l].astype(BF16), final_g[None], l == depth - 1)
    return x2d.reshape(b, s, d)
```

```python
import functools
import math

import jax
import jax.numpy as jnp
from jax import lax
from jax.experimental import pallas as pl
from jax.experimental.pallas import tpu as pltpu

CHUNK = 64
H_A = 4
H_R = 4
NUM_BUCKETS = 32
MAX_DISTANCE = 128
EPS = 1e-6
NEG_INF = -1e30

LANES = 128
ROW_TILE = 512
COL_TILE = 512
ATT_TQ = 256
ATT_TK = 256
QK_AHEAD = 2
REC_TS = 512
REC_GROUP = 2
SUB = 16
EXP_CLAMP = 80.0
ONES_ROWS = 16
V_ROWS = LANES + ONES_ROWS
LOG2E = 1.4426950408889634
VMEM_LIMIT = 56 * 1024 * 1024

F32 = jnp.float32
BF16 = jnp.bfloat16


def _rms(x, g):
    return x * lax.rsqrt(jnp.mean(x * x, axis=-1, keepdims=True) + EPS) * g


def _inproj_kernel(x_ref, g_ref, wqt_ref, wk_ref, wvt_ref, wr_ref,
                   zqt_ref, zk_ref, zvt_ref, zr_ref, *, q_scale):
    h = _rms(x_ref[...], g_ref[...]).astype(BF16)
    nt = (((1,), (1,)), ((), ()))
    zqt_ref[...] = (lax.dot_general(wqt_ref[...], h, nt, preferred_element_type=F32)
                    * q_scale).astype(BF16)
    zk_ref[...] = jnp.dot(h, wk_ref[...], preferred_element_type=F32).astype(BF16)
    zvt = lax.dot_general(wvt_ref[...], h, nt, preferred_element_type=F32).astype(BF16)
    ones = jnp.ones((ONES_ROWS, ATT_TK), BF16)
    for c in range(zvt_ref.shape[0]):
        for hd in range(H_A):
            zvt_ref[c, hd * V_ROWS:hd * V_ROWS + LANES, :] = (
                zvt[hd * LANES:(hd + 1) * LANES, c * ATT_TK:(c + 1) * ATT_TK])
            zvt_ref[c, hd * V_ROWS + LANES:(hd + 1) * V_ROWS, :] = ones
    for j in range(wr_ref.shape[1] // COL_TILE):
        cols = slice(j * COL_TILE, (j + 1) * COL_TILE)
        zr_ref[:, cols] = jnp.dot(h, wr_ref[:, cols], preferred_element_type=F32)


def _inproj(x2d, g, wqt, wk, wvt, wr, q_scale):
    n, d = x2d.shape
    att_w = wk.shape[1]
    rec_cols = wr.shape[1]
    const = lambda shape: pl.BlockSpec(shape, lambda i: (0,) * len(shape),
                                       pipeline_mode=pl.Buffered(1))
    return pl.pallas_call(
        functools.partial(_inproj_kernel, q_scale=q_scale),
        grid=(n // ROW_TILE,),
        in_specs=[
            pl.BlockSpec((ROW_TILE, d), lambda i: (i, 0)),
            const((1, d)), const((att_w, d)), const((d, att_w)), const((att_w, d)),
            const((d, rec_cols)),
        ],
        out_specs=[
            pl.BlockSpec((att_w, ROW_TILE), lambda i: (0, i)),
            pl.BlockSpec((ROW_TILE, att_w), lambda i: (i, 0)),
            pl.BlockSpec((ROW_TILE // ATT_TK, H_A * V_ROWS, ATT_TK), lambda i: (i, 0, 0)),
            pl.BlockSpec((ROW_TILE, rec_cols), lambda i: (i, 0)),
        ],
        out_shape=[
            jax.ShapeDtypeStruct((att_w, n), BF16),
            jax.ShapeDtypeStruct((n, att_w), BF16),
            jax.ShapeDtypeStruct((n // ATT_TK, H_A * V_ROWS, ATT_TK), BF16),
            jax.ShapeDtypeStruct((n, rec_cols), F32),
        ],
        compiler_params=pltpu.CompilerParams(
            dimension_semantics=("parallel",), vmem_limit_bytes=VMEM_LIMIT),
        name="inproj",
    )(x2d, g, wqt, wk, wvt, wr)


def _t5_bucket(rel):
    n_half = NUM_BUCKETS // 2
    max_exact = n_half // 2
    ret = jnp.where(rel > 0, n_half, 0)
    n = jnp.abs(rel)
    nf = jnp.maximum(n, 1).astype(jnp.float32)
    large = max_exact + (jnp.log(nf / max_exact) / math.log(MAX_DISTANCE / max_exact)
                         * (n_half - max_exact)).astype(jnp.int32)
    large = jnp.minimum(large, n_half - 1)
    return ret + jnp.where(n < max_exact, n, large)


def _bias_kernel(rb_ref, idx_ref, o_ref):
    h = pl.program_id(0)
    far = rb_ref[NUM_BUCKETS // 2 - 1, h]
    for t in range(2):
        idx = idx_ref[t]
        acc = jnp.zeros(idx.shape, F32)
        for b in range(NUM_BUCKETS):
            acc = jnp.where(idx == b, (rb_ref[b, h] - far) * LOG2E, acc)
        if t == 0:
            krow = lax.broadcasted_iota(jnp.int32, idx.shape, 0)
            qcol = lax.broadcasted_iota(jnp.int32, idx.shape, 1)
            acc = jnp.where(krow // CHUNK <= qcol // CHUNK, acc, NEG_INF)
        o_ref[0, t] = jnp.concatenate([acc, acc], axis=1)


def _bias_tiles(rel_bias):
    assert ATT_TQ == ATT_TK and ATT_TQ >= MAX_DISTANCE and ATT_TQ % CHUNK == 0
    kpos = jnp.arange(ATT_TK)[:, None]
    qpos = jnp.arange(ATT_TQ)[None, :]
    idx = jnp.stack([_t5_bucket(kpos - qpos), _t5_bucket(kpos - ATT_TK - qpos)]).astype(jnp.int32)
    return pl.pallas_call(
        _bias_kernel,
        grid=(H_A,),
        in_specs=[
            pl.BlockSpec(memory_space=pltpu.SMEM),
            pl.BlockSpec((2, ATT_TK, ATT_TQ), lambda h: (0, 0, 0)),
        ],
        out_specs=pl.BlockSpec((1, 2, ATT_TK, 2 * ATT_TQ), lambda h: (h, 0, 0, 0)),
        out_shape=jax.ShapeDtypeStruct((H_A, 2, ATT_TK, 2 * ATT_TQ), F32),
        name="bias_tiles",
    )(rel_bias.astype(F32), idx)


def _attn_kernel(lam_ref, qt_ref, k_ref, vt_ref, b_ref, g_ref, o_ref,
                 qq_ref, m_ref, acc_ref, *, out_scale):
    qi = pl.program_id(1)
    tq = qt_ref.shape[1]
    half = LANES // 2

    zero = jnp.zeros((half, tq), BF16)
    for h in range(H_A):
        qt = qt_ref[h * LANES:(h + 1) * LANES, :]
        qq_ref[h] = jnp.concatenate(
            [jnp.concatenate([qt[:half], zero], axis=0),
             jnp.concatenate([zero, qt[half:]], axis=0)], axis=1)
    m_ref[...] = jnp.full(m_ref.shape, NEG_INF, F32)
    acc_ref[...] = jnp.zeros(acc_ref.shape, F32)

    def scores(j, h):
        start = pl.multiple_of(j * ATT_TK, ATT_TK)
        kj = k_ref[0, pl.ds(start, ATT_TK), h * LANES:(h + 1) * LANES]
        return jnp.dot(kj, qq_ref[h], preferred_element_type=F32)

    def update(j, h, s, bias_tile):
        if bias_tile is not None:
            s = s + b_ref[h, bias_tile]
        m_prev = m_ref[h]
        m_next = jnp.maximum(m_prev, jnp.max(s, axis=0, keepdims=True))
        p = jnp.exp2(s - m_next).astype(BF16)
        alpha = jnp.exp2(m_prev - m_next)
        vj = vt_ref[j, h * V_ROWS:(h + 1) * V_ROWS, :]
        acc_ref[h] = alpha * acc_ref[h] + jnp.dot(vj, p, preferred_element_type=F32)
        m_ref[h] = m_next

    def run(blocks):
        items = [(j, h, bt) for j, bt in blocks for h in range(H_A)]
        pending = [scores(j, h) for j, h, _ in items[:QK_AHEAD]]
        for n, (j, h, bt) in enumerate(items):
            if n + QK_AHEAD < len(items):
                jn, hn, _ = items[n + QK_AHEAD]
                pending.append(scores(jn, hn))
            update(j, h, pending.pop(0), bt)

    n_far = jnp.maximum(qi - 1, 0)

    def far_pair(i, carry):
        run([(2 * i, None), (2 * i + 1, None)])
        return carry

    lax.fori_loop(0, n_far // 2, far_pair, 0)

    @pl.when(n_far % 2 == 1)
    def _():
        run([(n_far - 1, None)])

    @pl.when(qi >= 1)
    def _():
        run([(qi - 1, 1), (qi, 0)])

    @pl.when(qi == 0)
    def _():
        run([(qi, 0)])

    for h in range(H_A):
        acc = acc_ref[h]
        ot = acc[:LANES] / acc[LANES:LANES + 1]
        o = jnp.transpose(ot[:, :tq] - lam_ref[0] * ot[:, tq:])
        o_ref[0, :, h * LANES:(h + 1) * LANES] = (
            _rms(o, g_ref[...]) * out_scale).astype(o_ref.dtype)


def _attention(zqt, zk, zvt, bias, lam, g, lam_init, b, s):
    att_w = zk.shape[1]
    nq = s // ATT_TQ
    nk = s // ATT_TK
    kernel = functools.partial(_attn_kernel, out_scale=1.0 - lam_init)
    return pl.pallas_call(
        kernel,
        grid=(b, nq),
        in_specs=[
            pl.BlockSpec(memory_space=pltpu.SMEM),
            pl.BlockSpec((att_w, ATT_TQ), lambda bi, i: (0, bi * nq + i)),
            pl.BlockSpec((1, s, att_w), lambda bi, i: (bi, 0, 0)),
            pl.BlockSpec((nk, H_A * V_ROWS, ATT_TK), lambda bi, i: (bi, 0, 0)),
            pl.BlockSpec((H_A, 2, ATT_TK, 2 * ATT_TQ), lambda bi, i: (0, 0, 0, 0),
                         pipeline_mode=pl.Buffered(1)),
            pl.BlockSpec((1, LANES), lambda bi, i: (0, 0)),
        ],
        out_specs=pl.BlockSpec((1, ATT_TQ, att_w), lambda bi, i: (bi, i, 0)),
        out_shape=jax.ShapeDtypeStruct((b, s, att_w), BF16),
        scratch_shapes=[
            pltpu.VMEM((H_A, LANES, 2 * ATT_TQ), BF16),
            pltpu.VMEM((H_A, 1, 2 * ATT_TQ), F32),
            pltpu.VMEM((H_A, V_ROWS, 2 * ATT_TQ), F32),
        ],
        compiler_params=pltpu.CompilerParams(
            dimension_semantics=("parallel", "arbitrary"), vmem_limit_bytes=VMEM_LIMIT),
        name="diff_attention",
    )(lam, zqt, zk.reshape(b, s, att_w), zvt, bias, g)


def _cumsum_rows(g, tri3):
    hi = g.astype(BF16)
    r1 = g - hi.astype(F32)
    mid = r1.astype(BF16)
    lo = (r1 - mid.astype(F32)).astype(BF16)
    return jnp.dot(tri3, jnp.concatenate([hi, mid, lo], axis=0), preferred_element_type=F32)


def _rec_group(row_slices, rq_ref, rf_ref, ri_ref, rg_ref, llb_ref, l1m_ref, g_ref, o_ref,
               st_ref):
    c = CHUNK
    nt = (((1,), (1,)), ((), ()))
    tn = (((0,), (0,)), ((), ()))
    heads = [slice(h * LANES, (h + 1) * LANES) for h in range(H_R)]
    row = lax.broadcasted_iota(jnp.int32, (c, c), 0)
    col = lax.broadcasted_iota(jnp.int32, (c, c), 1)
    causal = col <= row
    tri = causal.astype(BF16)
    tri3 = jnp.concatenate([tri, tri, tri], axis=1)
    log_lb = llb_ref[...]
    log_1mlb = l1m_ref[...]

    chunks = []
    for rows in row_slices:
        rq = rq_ref[0, rows, :]
        rf = rf_ref[0, rows, :]
        ls = jnp.minimum(rf, 0.0) - jnp.log(1.0 + jnp.exp(-jnp.abs(rf)))
        bterm = log_1mlb + ls
        g = jnp.maximum(log_lb, bterm) + jnp.log(1.0 + jnp.exp(-jnp.abs(log_lb - bterm)))
        chunks.append(dict(g=g, kin=jnp.exp(bterm - rf), q=rq / (1.0 + jnp.exp(-rq)),
                           v=ri_ref[0, rows, :].astype(BF16)))
    for d in chunks:
        d["bc"] = _cumsum_rows(d["g"], tri3)

    for d in chunks:
        bc, q, kin = d["bc"], d["q"], d["kin"]
        blast = bc[c - 1:c]
        d["qe"] = (q * jnp.exp(bc)).astype(BF16)
        d["ke"] = (kin * jnp.exp(blast - bc)).astype(BF16)
        d["dl"] = jnp.exp(blast)
        d["lhs"], d["rhs"] = [], []
        for i in range(c // SUB):
            lo, hi = i * SUB, (i + 1) * SUB
            mid = lo + SUB // 2
            u = jnp.clip(bc[lo:hi] - bc[mid:mid + 1], -EXP_CLAMP, EXP_CLAMP)
            qd = (q[lo:hi] * jnp.exp(u)).astype(BF16)
            kd = (kin[lo:hi] * jnp.exp(-u)).astype(BF16)
            kd_pad = jnp.concatenate(
                ([jnp.zeros((lo, kd.shape[1]), BF16)] if lo else []) + [kd]
                + ([jnp.zeros((c - hi, kd.shape[1]), BF16)] if c - hi else []), axis=0)
            if i == 0:
                d["lhs"].append((qd,))
                d["rhs"].append((kd_pad,))
            else:
                ref = bc[lo:lo + 1]
                qo = (q[lo:hi] * jnp.exp(bc[lo:hi] - ref)).astype(BF16)
                ko = (kin[:lo] * jnp.exp(ref - bc[:lo])).astype(BF16)
                ko_pad = jnp.concatenate([ko, jnp.zeros((c - lo, ko.shape[1]), BF16)], axis=0)
                d["lhs"].append((qo, qd))
                d["rhs"].append((ko_pad, kd_pad))

    for d in chunks:
        d["inc"] = [lax.dot_general(d["v"][:, hs], d["ke"][:, hs], tn, preferred_element_type=F32)
                    for hs in heads]
    for d in chunks:
        d["a"] = []
        for hs in heads:
            a_rows = [lax.dot_general(jnp.concatenate([p[:, hs] for p in lhs], axis=1),
                                      jnp.concatenate([p[:, hs] for p in rhs], axis=1),
                                      nt, preferred_element_type=F32)
                      for lhs, rhs in zip(d["lhs"], d["rhs"])]
            d["a"].append(jnp.where(causal, jnp.concatenate(a_rows, axis=0), 0.0).astype(BF16))
    states = [st_ref[h] for h in range(H_R)]
    for d in chunks:
        d["inter"] = [lax.dot_general(d["qe"][:, hs], states[h].astype(BF16), nt,
                                      preferred_element_type=F32)
                      for h, hs in enumerate(heads)]
        states = [states[h] * d["dl"][:, hs] + d["inc"][h] for h, hs in enumerate(heads)]
    for h in range(H_R):
        st_ref[h] = states[h]
    for rows, d in zip(row_slices, chunks):
        rg = rg_ref[0, rows, :]
        gate = rg / (1.0 + jnp.exp(-rg))
        for h, hs in enumerate(heads):
            o = d["inter"][h] + jnp.dot(d["a"][h], d["v"][:, hs], preferred_element_type=F32)
            o_ref[0, rows, hs] = (_rms(o, g_ref[...]) * gate[:, hs]).astype(o_ref.dtype)


def _rec_kernel(rq_ref, rf_ref, ri_ref, rg_ref, llb_ref, l1m_ref, g_ref, o_ref, st_ref):
    @pl.when(pl.program_id(1) == 0)
    def _():
        st_ref[...] = jnp.zeros(st_ref.shape, F32)

    group = REC_GROUP * CHUNK

    def body(gi, carry):
        r0 = pl.multiple_of(gi * group, group)
        _rec_group([pl.ds(r0 + n * CHUNK, CHUNK) for n in range(REC_GROUP)],
                   rq_ref, rf_ref, ri_ref, rg_ref, llb_ref, l1m_ref, g_ref, o_ref, st_ref)
        return carry

    lax.fori_loop(0, rq_ref.shape[1] // group, body, 0)


def _recurrence(zr, log_lb, log_1mlb, g):
    b, s, cols4 = zr.shape
    w = cols4 // 4
    spec = lambda j: pl.BlockSpec((1, REC_TS, w), lambda bi, si, j=j: (bi, si, j))
    vec = lambda n: pl.BlockSpec((1, n), lambda bi, si: (0, 0))
    return pl.pallas_call(
        _rec_kernel,
        grid=(b, s // REC_TS),
        in_specs=[spec(0), spec(1), spec(2), spec(3), vec(w), vec(w), vec(LANES)],
        out_specs=pl.BlockSpec((1, REC_TS, w), lambda bi, si: (bi, si, 0)),
        out_shape=jax.ShapeDtypeStruct((b, s, w), BF16),
        scratch_shapes=[pltpu.VMEM((H_R, LANES, LANES), F32)],
        compiler_params=pltpu.CompilerParams(
            dimension_semantics=("parallel", "arbitrary"), vmem_limit_bytes=VMEM_LIMIT),
        name="hgrn2",
    )(zr, zr, zr, zr, log_lb, log_1mlb, g)


def _mlp_kernel(x_ref, oa_ref, or_ref, wo_ref, g2_ref, wu_ref, wd_ref, gf_ref, o_ref, u2_ref,
                *, final_norm):
    wa = oa_ref.shape[1]
    x = x_ref[...]
    x = x + jnp.dot(oa_ref[...], wo_ref[:wa, :], preferred_element_type=F32)
    x = x + jnp.dot(or_ref[...], wo_ref[wa:, :], preferred_element_type=F32)
    h2 = _rms(x, g2_ref[...]).astype(BF16)
    d_ff = wu_ref.shape[1]
    for f in range(d_ff // COL_TILE):
        cols = slice(f * COL_TILE, (f + 1) * COL_TILE)
        u = jnp.maximum(jnp.dot(h2, wu_ref[:, cols], preferred_element_type=F32), 0.0)
        u2_ref[:, cols] = (u * u).astype(BF16)
    x = x + jnp.dot(u2_ref[...], wd_ref[...], preferred_element_type=F32)
    if final_norm:
        x = _rms(x, gf_ref[...])
    o_ref[...] = x


def _mlp(x2d, oa, orr, wo, g2, wu, wd, gf, final_norm):
    n, d = x2d.shape
    wa, wr = oa.shape[1], orr.shape[1]
    d_ff = wu.shape[1]
    const = lambda shape: pl.BlockSpec(shape, lambda i: (0, 0), pipeline_mode=pl.Buffered(1))
    return pl.pallas_call(
        functools.partial(_mlp_kernel, final_norm=final_norm),
        grid=(n // ROW_TILE,),
        in_specs=[
            pl.BlockSpec((ROW_TILE, d), lambda i: (i, 0)),
            pl.BlockSpec((ROW_TILE, wa), lambda i: (i, 0)),
            pl.BlockSpec((ROW_TILE, wr), lambda i: (i, 0)),
            const((wa + wr, d)),
            const((1, d)),
            const((d, d_ff)),
            const((d_ff, d)),
            const((1, d)),
        ],
        out_specs=pl.BlockSpec((ROW_TILE, d), lambda i: (i, 0)),
        out_shape=jax.ShapeDtypeStruct((n, d), F32),
        scratch_shapes=[pltpu.VMEM((ROW_TILE, d_ff), BF16)],
        compiler_params=pltpu.CompilerParams(
            dimension_semantics=("parallel",), vmem_limit_bytes=VMEM_LIMIT),
        name="outproj_mlp",
    )(x2d, oa, orr, wo, g2, wu, wd, gf)


def kernel(x, norm1_g, w_in, lam_qk, attn_norm_g, lb_logits, hgrn_norm_g, w_out, norm2_g,
           w_up, w_down, rel_bias, final_g):
    b, s, d = x.shape
    depth = w_in.shape[0]
    att_w = attn_norm_g.shape[1] * H_A
    rec_w = hgrn_norm_g.shape[1] * H_R
    assert attn_norm_g.shape[1] == LANES and hgrn_norm_g.shape[1] == LANES
    assert w_in.shape[2] == 3 * att_w + 4 * rec_w
    assert (b * s) % ROW_TILE == 0 and s % ATT_TQ == 0 and s % REC_TS == 0

    lb = jnp.cumsum(jax.nn.softmax(lb_logits.astype(F32), axis=0), axis=0)
    lb = lb - lb[0:1]
    log_lb = jnp.log(lb)
    log_1mlb = jnp.log1p(-lb)

    bias = _bias_tiles(rel_bias)
    x2d = x.reshape(b * s, d)
    q_scale = (att_w // (2 * H_A)) ** -0.5 * LOG2E
    for l in range(depth):
        w = w_in[l].astype(BF16)
        zqt, zk, zvt, zr = _inproj(
            x2d, norm1_g[l][None], w[:, :att_w].T, w[:, att_w:2 * att_w],
            w[:, 2 * att_w:3 * att_w].T, w[:, 3 * att_w:], q_scale)

        lam_init = 0.8 - 0.6 * math.exp(-0.3 * l)
        lq = lam_qk[l].astype(F32)
        lam = jnp.exp(jnp.sum(lq[0] * lq[1])) - jnp.exp(jnp.sum(lq[2] * lq[3])) + lam_init
        oa = _attention(zqt, zk, zvt, bias, lam.reshape(1), attn_norm_g[l][None].astype(F32),
                        lam_init, b, s)
        orr = _recurrence(zr.reshape(b, s, 4 * rec_w), log_lb[l][None], log_1mlb[l][None],
                          hgrn_norm_g[l][None].astype(F32))
        x2d = _mlp(x2d, oa.reshape(b * s, att_w), orr.reshape(b * s, rec_w),
                   w_out[l].astype(BF16), norm2_g[l][None], w_up[l].astype(BF16),
                   w_down[l].astype(BF16), final_g[None], l == depth - 1)
    return x2d.reshape(b, s, d)
```

```python
import functools
import math

import jax
import jax.numpy as jnp
from jax import lax
from jax.experimental import pallas as pl
from jax.experimental.pallas import tpu as pltpu

CHUNK = 64
H_A = 4
H_R = 4
NUM_BUCKETS = 32
MAX_DISTANCE = 128
EPS = 1e-6
NEG_INF = -1e30

LANES = 128
ROW_TILE = 512
COL_TILE = 512
ATT_TQ = 256
ATT_TK = 256
QK_AHEAD = 2
REC_TS = 512
REC_GROUP = 2
SUB = 16
EXP_CLAMP = 80.0
ONES_ROWS = 16
V_ROWS = LANES + ONES_ROWS
LOG2E = 1.4426950408889634
VMEM_LIMIT = 56 * 1024 * 1024

F32 = jnp.float32
BF16 = jnp.bfloat16


def _rms(x, g):
    return x * lax.rsqrt(jnp.mean(x * x, axis=-1, keepdims=True) + EPS) * g


def _inproj_kernel(x_ref, g_ref, wqt_ref, wk_ref, wvt_ref, wr_ref,
                   zqt_ref, zk_ref, zvt_ref, zr_ref, *, q_scale):
    h = _rms(x_ref[...], g_ref[...]).astype(BF16)
    nt = (((1,), (1,)), ((), ()))
    zqt_ref[...] = (lax.dot_general(wqt_ref[...], h, nt, preferred_element_type=F32)
                    * q_scale).astype(BF16)
    zk_ref[...] = jnp.dot(h, wk_ref[...], preferred_element_type=F32).astype(BF16)
    zvt = lax.dot_general(wvt_ref[...], h, nt, preferred_element_type=F32).astype(BF16)
    ones = jnp.ones((ONES_ROWS, ATT_TK), BF16)
    for c in range(zvt_ref.shape[0]):
        for hd in range(H_A):
            zvt_ref[c, hd * V_ROWS:hd * V_ROWS + LANES, :] = (
                zvt[hd * LANES:(hd + 1) * LANES, c * ATT_TK:(c + 1) * ATT_TK])
            zvt_ref[c, hd * V_ROWS + LANES:(hd + 1) * V_ROWS, :] = ones
    for j in range(wr_ref.shape[1] // COL_TILE):
        cols = slice(j * COL_TILE, (j + 1) * COL_TILE)
        zr_ref[:, cols] = jnp.dot(h, wr_ref[:, cols], preferred_element_type=F32)


def _inproj(x2d, g, wqt, wk, wvt, wr, q_scale):
    n, d = x2d.shape
    att_w = wk.shape[1]
    rec_cols = wr.shape[1]
    const = lambda shape: pl.BlockSpec(shape, lambda i: (0,) * len(shape),
                                       pipeline_mode=pl.Buffered(1))
    return pl.pallas_call(
        functools.partial(_inproj_kernel, q_scale=q_scale),
        grid=(n // ROW_TILE,),
        in_specs=[
            pl.BlockSpec((ROW_TILE, d), lambda i: (i, 0)),
            const((1, d)), const((att_w, d)), const((d, att_w)), const((att_w, d)),
            const((d, rec_cols)),
        ],
        out_specs=[
            pl.BlockSpec((att_w, ROW_TILE), lambda i: (0, i)),
            pl.BlockSpec((ROW_TILE, att_w), lambda i: (i, 0)),
            pl.BlockSpec((ROW_TILE // ATT_TK, H_A * V_ROWS, ATT_TK), lambda i: (i, 0, 0)),
            pl.BlockSpec((ROW_TILE, rec_cols), lambda i: (i, 0)),
        ],
        out_shape=[
            jax.ShapeDtypeStruct((att_w, n), BF16),
            jax.ShapeDtypeStruct((n, att_w), BF16),
            jax.ShapeDtypeStruct((n // ATT_TK, H_A * V_ROWS, ATT_TK), BF16),
            jax.ShapeDtypeStruct((n, rec_cols), F32),
        ],
        compiler_params=pltpu.CompilerParams(
            dimension_semantics=("parallel",), vmem_limit_bytes=VMEM_LIMIT),
        name="inproj",
    )(x2d, g, wqt, wk, wvt, wr)


def _t5_bucket(rel):
    n_half = NUM_BUCKETS // 2
    max_exact = n_half // 2
    ret = jnp.where(rel > 0, n_half, 0)
    n = jnp.abs(rel)
    nf = jnp.maximum(n, 1).astype(jnp.float32)
    large = max_exact + (jnp.log(nf / max_exact) / math.log(MAX_DISTANCE / max_exact)
                         * (n_half - max_exact)).astype(jnp.int32)
    large = jnp.minimum(large, n_half - 1)
    return ret + jnp.where(n < max_exact, n, large)


def _bias_kernel(rb_ref, idx_ref, o_ref):
    h = pl.program_id(0)
    far = rb_ref[NUM_BUCKETS // 2 - 1, h]
    for t in range(2):
        idx = idx_ref[t]
        acc = jnp.zeros(idx.shape, F32)
        for b in range(NUM_BUCKETS):
            acc = jnp.where(idx == b, (rb_ref[b, h] - far) * LOG2E, acc)
        if t == 0:
            krow = lax.broadcasted_iota(jnp.int32, idx.shape, 0)
            qcol = lax.broadcasted_iota(jnp.int32, idx.shape, 1)
            acc = jnp.where(krow // CHUNK <= qcol // CHUNK, acc, NEG_INF)
        o_ref[0, t] = jnp.concatenate([acc, acc], axis=1)


def _bias_tiles(rel_bias):
    assert ATT_TQ == ATT_TK and ATT_TQ % MAX_DISTANCE == 0 and ATT_TQ % CHUNK == 0
    assert MAX_DISTANCE % LANES == 0
    kpos = jnp.arange(ATT_TK)[:, None]
    qpos = jnp.arange(ATT_TQ)[None, :]
    idx = jnp.stack([_t5_bucket(kpos - qpos), _t5_bucket(kpos - ATT_TK - qpos)]).astype(jnp.int32)
    return pl.pallas_call(
        _bias_kernel,
        grid=(H_A,),
        in_specs=[
            pl.BlockSpec(memory_space=pltpu.SMEM),
            pl.BlockSpec((2, ATT_TK, ATT_TQ), lambda h: (0, 0, 0)),
        ],
        out_specs=pl.BlockSpec((1, 2, ATT_TK, 2 * ATT_TQ), lambda h: (h, 0, 0, 0)),
        out_shape=jax.ShapeDtypeStruct((H_A, 2, ATT_TK, 2 * ATT_TQ), F32),
        name="bias_tiles",
    )(rel_bias.astype(F32), idx)


def _attn_kernel(lam_ref, qt_ref, k_ref, vt_ref, b_ref, g_ref, o_ref,
                 qq_ref, m_ref, acc_ref, *, out_scale):
    qi = pl.program_id(1)
    tq = qt_ref.shape[1]
    half = LANES // 2

    zero = jnp.zeros((half, tq), BF16)
    for h in range(H_A):
        qt = qt_ref[h * LANES:(h + 1) * LANES, :]
        qq_ref[h] = jnp.concatenate(
            [jnp.concatenate([qt[:half], zero], axis=0),
             jnp.concatenate([zero, qt[half:]], axis=0)], axis=1)
    m_ref[...] = jnp.full(m_ref.shape, NEG_INF, F32)
    acc_ref[...] = jnp.zeros(acc_ref.shape, F32)

    def scores(j, h):
        start = pl.multiple_of(j * ATT_TK, ATT_TK)
        kj = k_ref[0, pl.ds(start, ATT_TK), h * LANES:(h + 1) * LANES]
        return jnp.dot(kj, qq_ref[h], preferred_element_type=F32)

    def biased(x, h, bias_tile):
        if bias_tile is None:
            return x
        if bias_tile == 0:
            return x + b_ref[h, 0]
        z = ATT_TK - MAX_DISTANCE
        pieces = []
        for c0 in range(0, 2 * tq, MAX_DISTANCE):
            blk = x[z:, c0:c0 + MAX_DISTANCE]
            if c0 % tq == 0:
                blk = blk + b_ref[h, 1, z:, c0:c0 + MAX_DISTANCE]
            pieces.append(blk)
        return jnp.concatenate([x[:z], jnp.concatenate(pieces, axis=1)], axis=0)

    def update(j, h, s, bias_tile):
        m_prev = m_ref[h]
        m_next = jnp.maximum(m_prev, jnp.max(biased(s, h, bias_tile), axis=0, keepdims=True))
        p = jnp.exp2(biased(s - m_next, h, bias_tile)).astype(BF16)
        alpha = jnp.exp2(m_prev - m_next)
        vj = vt_ref[j, h * V_ROWS:(h + 1) * V_ROWS, :]
        acc_ref[h] = alpha * acc_ref[h] + jnp.dot(vj, p, preferred_element_type=F32)
        m_ref[h] = m_next

    def run(blocks):
        items = [(j, h, bt) for j, bt in blocks for h in range(H_A)]
        pending = [scores(j, h) for j, h, _ in items[:QK_AHEAD]]
        for n, (j, h, bt) in enumerate(items):
            if n + QK_AHEAD < len(items):
                jn, hn, _ = items[n + QK_AHEAD]
                pending.append(scores(jn, hn))
            update(j, h, pending.pop(0), bt)

    n_far = jnp.maximum(qi - 1, 0)

    def far_pair(i, carry):
        run([(2 * i, None), (2 * i + 1, None)])
        return carry

    lax.fori_loop(0, n_far // 2, far_pair, 0)

    @pl.when(n_far % 2 == 1)
    def _():
        run([(n_far - 1, None)])

    @pl.when(qi >= 1)
    def _():
        run([(qi - 1, 1), (qi, 0)])

    @pl.when(qi == 0)
    def _():
        run([(qi, 0)])

    for h in range(H_A):
        acc = acc_ref[h]
        ot = acc[:LANES] / acc[LANES:LANES + 1]
        o = jnp.transpose(ot[:, :tq] - lam_ref[0] * ot[:, tq:])
        o_ref[0, :, h * LANES:(h + 1) * LANES] = (
            _rms(o, g_ref[...]) * out_scale).astype(o_ref.dtype)


def _attention(zqt, zk, zvt, bias, lam, g, lam_init, b, s):
    att_w = zk.shape[1]
    nq = s // ATT_TQ
    nk = s // ATT_TK
    kernel = functools.partial(_attn_kernel, out_scale=1.0 - lam_init)
    return pl.pallas_call(
        kernel,
        grid=(b, nq),
        in_specs=[
            pl.BlockSpec(memory_space=pltpu.SMEM),
            pl.BlockSpec((att_w, ATT_TQ), lambda bi, i: (0, bi * nq + i)),
            pl.BlockSpec((1, s, att_w), lambda bi, i: (bi, 0, 0)),
            pl.BlockSpec((nk, H_A * V_ROWS, ATT_TK), lambda bi, i: (bi, 0, 0)),
            pl.BlockSpec((H_A, 2, ATT_TK, 2 * ATT_TQ), lambda bi, i: (0, 0, 0, 0),
                         pipeline_mode=pl.Buffered(1)),
            pl.BlockSpec((1, LANES), lambda bi, i: (0, 0)),
        ],
        out_specs=pl.BlockSpec((1, ATT_TQ, att_w), lambda bi, i: (bi, i, 0)),
        out_shape=jax.ShapeDtypeStruct((b, s, att_w), BF16),
        scratch_shapes=[
            pltpu.VMEM((H_A, LANES, 2 * ATT_TQ), BF16),
            pltpu.VMEM((H_A, 1, 2 * ATT_TQ), F32),
            pltpu.VMEM((H_A, V_ROWS, 2 * ATT_TQ), F32),
        ],
        compiler_params=pltpu.CompilerParams(
            dimension_semantics=("parallel", "arbitrary"), vmem_limit_bytes=VMEM_LIMIT),
        name="diff_attention",
    )(lam, zqt, zk.reshape(b, s, att_w), zvt, bias, g)


def _cumsum_rows(g, tri3):
    hi = g.astype(BF16)
    r1 = g - hi.astype(F32)
    mid = r1.astype(BF16)
    lo = (r1 - mid.astype(F32)).astype(BF16)
    return jnp.dot(tri3, jnp.concatenate([hi, mid, lo], axis=0), preferred_element_type=F32)


def _rec_group(row_slices, rq_ref, rf_ref, ri_ref, rg_ref, llb_ref, l1m_ref, g_ref, o_ref,
               st_ref):
    c = CHUNK
    nt = (((1,), (1,)), ((), ()))
    tn = (((0,), (0,)), ((), ()))
    heads = [slice(h * LANES, (h + 1) * LANES) for h in range(H_R)]
    row = lax.broadcasted_iota(jnp.int32, (c, c), 0)
    col = lax.broadcasted_iota(jnp.int32, (c, c), 1)
    causal = col <= row
    tri = causal.astype(BF16)
    tri3 = jnp.concatenate([tri, tri, tri], axis=1)
    log_lb = llb_ref[...]
    log_1mlb = l1m_ref[...]

    chunks = []
    for rows in row_slices:
        rq = rq_ref[0, rows, :]
        rf = rf_ref[0, rows, :]
        ls = jnp.minimum(rf, 0.0) - jnp.log(1.0 + jnp.exp(-jnp.abs(rf)))
        bterm = log_1mlb + ls
        g = jnp.maximum(log_lb, bterm) + jnp.log(1.0 + jnp.exp(-jnp.abs(log_lb - bterm)))
        chunks.append(dict(g=g, kin=jnp.exp(bterm - rf), q=rq / (1.0 + jnp.exp(-rq)),
                           v=ri_ref[0, rows, :].astype(BF16)))
    for d in chunks:
        d["bc"] = _cumsum_rows(d["g"], tri3)

    for d in chunks:
        bc, q, kin = d["bc"], d["q"], d["kin"]
        blast = bc[c - 1:c]
        d["qe"] = (q * jnp.exp(bc)).astype(BF16)
        d["ke"] = (kin * jnp.exp(blast - bc)).astype(BF16)
        d["dl"] = jnp.exp(blast)
        d["lhs"], d["rhs"] = [], []
        for i in range(c // SUB):
            lo, hi = i * SUB, (i + 1) * SUB
            mid = lo + SUB // 2
            u = jnp.clip(bc[lo:hi] - bc[mid:mid + 1], -EXP_CLAMP, EXP_CLAMP)
            qd = (q[lo:hi] * jnp.exp(u)).astype(BF16)
            kd = (kin[lo:hi] * jnp.exp(-u)).astype(BF16)
            kd_pad = jnp.concatenate(
                ([jnp.zeros((lo, kd.shape[1]), BF16)] if lo else []) + [kd]
                + ([jnp.zeros((c - hi, kd.shape[1]), BF16)] if c - hi else []), axis=0)
            if i == 0:
                d["lhs"].append((qd,))
                d["rhs"].append((kd_pad,))
            else:
                ref = bc[lo:lo + 1]
                qo = (q[lo:hi] * jnp.exp(bc[lo:hi] - ref)).astype(BF16)
                ko = (kin[:lo] * jnp.exp(ref - bc[:lo])).astype(BF16)
                ko_pad = jnp.concatenate([ko, jnp.zeros((c - lo, ko.shape[1]), BF16)], axis=0)
                d["lhs"].append((qo, qd))
                d["rhs"].append((ko_pad, kd_pad))

    for d in chunks:
        d["inc"] = [lax.dot_general(d["v"][:, hs], d["ke"][:, hs], tn, preferred_element_type=F32)
                    for hs in heads]
    for d in chunks:
        d["a"] = []
        for hs in heads:
            a_rows = [lax.dot_general(jnp.concatenate([p[:, hs] for p in lhs], axis=1),
                                      jnp.concatenate([p[:, hs] for p in rhs], axis=1),
                                      nt, preferred_element_type=F32)
                      for lhs, rhs in zip(d["lhs"], d["rhs"])]
            d["a"].append(jnp.where(causal, jnp.concatenate(a_rows, axis=0), 0.0).astype(BF16))
    states = [st_ref[h] for h in range(H_R)]
    for d in chunks:
        d["inter"] = [lax.dot_general(d["qe"][:, hs], states[h].astype(BF16), nt,
                                      preferred_element_type=F32)
                      for h, hs in enumerate(heads)]
        states = [states[h] * d["dl"][:, hs] + d["inc"][h] for h, hs in enumerate(heads)]
    for h in range(H_R):
        st_ref[h] = states[h]
    for rows, d in zip(row_slices, chunks):
        rg = rg_ref[0, rows, :]
        gate = rg / (1.0 + jnp.exp(-rg))
        for h, hs in enumerate(heads):
            o = d["inter"][h] + jnp.dot(d["a"][h], d["v"][:, hs], preferred_element_type=F32)
            o_ref[0, rows, hs] = (_rms(o, g_ref[...]) * gate[:, hs]).astype(o_ref.dtype)


def _rec_kernel(rq_ref, rf_ref, ri_ref, rg_ref, llb_ref, l1m_ref, g_ref, o_ref, st_ref):
    @pl.when(pl.program_id(1) == 0)
    def _():
        st_ref[...] = jnp.zeros(st_ref.shape, F32)

    group = REC_GROUP * CHUNK

    def body(gi, carry):
        r0 = pl.multiple_of(gi * group, group)
        _rec_group([pl.ds(r0 + n * CHUNK, CHUNK) for n in range(REC_GROUP)],
                   rq_ref, rf_ref, ri_ref, rg_ref, llb_ref, l1m_ref, g_ref, o_ref, st_ref)
        return carry

    lax.fori_loop(0, rq_ref.shape[1] // group, body, 0)


def _recurrence(zr, log_lb, log_1mlb, g):
    b, s, cols4 = zr.shape
    w = cols4 // 4
    spec = lambda j: pl.BlockSpec((1, REC_TS, w), lambda bi, si, j=j: (bi, si, j))
    vec = lambda n: pl.BlockSpec((1, n), lambda bi, si: (0, 0))
    return pl.pallas_call(
        _rec_kernel,
        grid=(b, s // REC_TS),
        in_specs=[spec(0), spec(1), spec(2), spec(3), vec(w), vec(w), vec(LANES)],
        out_specs=pl.BlockSpec((1, REC_TS, w), lambda bi, si: (bi, si, 0)),
        out_shape=jax.ShapeDtypeStruct((b, s, w), BF16),
        scratch_shapes=[pltpu.VMEM((H_R, LANES, LANES), F32)],
        compiler_params=pltpu.CompilerParams(
            dimension_semantics=("parallel", "arbitrary"), vmem_limit_bytes=VMEM_LIMIT),
        name="hgrn2",
    )(zr, zr, zr, zr, log_lb, log_1mlb, g)


def _mlp_kernel(x_ref, oa_ref, or_ref, wo_ref, g2_ref, wu_ref, wd_ref, gf_ref, o_ref, u2_ref,
                *, final_norm):
    wa = oa_ref.shape[1]
    x = x_ref[...]
    x = x + jnp.dot(oa_ref[...], wo_ref[:wa, :], preferred_element_type=F32)
    x = x + jnp.dot(or_ref[...], wo_ref[wa:, :], preferred_element_type=F32)
    h2 = _rms(x, g2_ref[...]).astype(BF16)
    d_ff = wu_ref.shape[1]
    for f in range(d_ff // COL_TILE):
        cols = slice(f * COL_TILE, (f + 1) * COL_TILE)
        u = jnp.maximum(jnp.dot(h2, wu_ref[:, cols], preferred_element_type=F32), 0.0)
        u2_ref[:, cols] = (u * u).astype(BF16)
    x = x + jnp.dot(u2_ref[...], wd_ref[...], preferred_element_type=F32)
    if final_norm:
        x = _rms(x, gf_ref[...])
    o_ref[...] = x


def _mlp(x2d, oa, orr, wo, g2, wu, wd, gf, final_norm):
    n, d = x2d.shape
    wa, wr = oa.shape[1], orr.shape[1]
    d_ff = wu.shape[1]
    const = lambda shape: pl.BlockSpec(shape, lambda i: (0, 0), pipeline_mode=pl.Buffered(1))
    return pl.pallas_call(
        functools.partial(_mlp_kernel, final_norm=final_norm),
        grid=(n // ROW_TILE,),
        in_specs=[
            pl.BlockSpec((ROW_TILE, d), lambda i: (i, 0)),
            pl.BlockSpec((ROW_TILE, wa), lambda i: (i, 0)),
            pl.BlockSpec((ROW_TILE, wr), lambda i: (i, 0)),
            const((wa + wr, d)),
            const((1, d)),
            const((d, d_ff)),
            const((d_ff, d)),
            const((1, d)),
        ],
        out_specs=pl.BlockSpec((ROW_TILE, d), lambda i: (i, 0)),
        out_shape=jax.ShapeDtypeStruct((n, d), F32),
        scratch_shapes=[pltpu.VMEM((ROW_TILE, d_ff), BF16)],
        compiler_params=pltpu.CompilerParams(
            dimension_semantics=("parallel",), vmem_limit_bytes=VMEM_LIMIT),
        name="outproj_mlp",
    )(x2d, oa, orr, wo, g2, wu, wd, gf)


def kernel(x, norm1_g, w_in, lam_qk, attn_norm_g, lb_logits, hgrn_norm_g, w_out, norm2_g,
           w_up, w_down, rel_bias, final_g):
    b, s, d = x.shape
    depth = w_in.shape[0]
    att_w = attn_norm_g.shape[1] * H_A
    rec_w = hgrn_norm_g.shape[1] * H_R
    assert attn_norm_g.shape[1] == LANES and hgrn_norm_g.shape[1] == LANES
    assert w_in.shape[2] == 3 * att_w + 4 * rec_w
    assert (b * s) % ROW_TILE == 0 and s % ATT_TQ == 0 and s % REC_TS == 0

    lb = jnp.cumsum(jax.nn.softmax(lb_logits.astype(F32), axis=0), axis=0)
    lb = lb - lb[0:1]
    log_lb = jnp.log(lb)
    log_1mlb = jnp.log1p(-lb)

    bias = _bias_tiles(rel_bias)
    x2d = x.reshape(b * s, d)
    q_scale = (att_w // (2 * H_A)) ** -0.5 * LOG2E
    for l in range(depth):
        w = w_in[l].astype(BF16)
        zqt, zk, zvt, zr = _inproj(
            x2d, norm1_g[l][None], w[:, :att_w].T, w[:, att_w:2 * att_w],
            w[:, 2 * att_w:3 * att_w].T, w[:, 3 * att_w:], q_scale)

        lam_init = 0.8 - 0.6 * math.exp(-0.3 * l)
        lq = lam_qk[l].astype(F32)
        lam = jnp.exp(jnp.sum(lq[0] * lq[1])) - jnp.exp(jnp.sum(lq[2] * lq[3])) + lam_init
        oa = _attention(zqt, zk, zvt, bias, lam.reshape(1), attn_norm_g[l][None].astype(F32),
                        lam_init, b, s)
        orr = _recurrence(zr.reshape(b, s, 4 * rec_w), log_lb[l][None], log_1mlb[l][None],
                          hgrn_norm_g[l][None].astype(F32))
        x2d = _mlp(x2d, oa.reshape(b * s, att_w), orr.reshape(b * s, rec_w),
                   w_out[l].astype(BF16), norm2_g[l][None], w_up[l].astype(BF16),
                   w_down[l].astype(BF16), final_g[None], l == depth - 1)
    return x2d.reshape(b, s, d)
```

```python
import functools
import math

import jax
import jax.numpy as jnp
from jax import lax
from jax.experimental import pallas as pl
from jax.experimental.pallas import tpu as pltpu

CHUNK = 64
H_A = 4
H_R = 4
NUM_BUCKETS = 32
MAX_DISTANCE = 128
EPS = 1e-6
NEG_INF = -1e30

LANES = 128
ROW_TILE = 512
COL_TILE = 512
ATT_TQ = 256
ATT_TK = 256
QK_AHEAD = 2
REC_TS = 512
REC_GROUP = 8
SUB = 16
LOG2E = 1.4426950408889634
EXP2_CLAMP = 80.0 * LOG2E
ONES_ROWS = 16
V_ROWS = LANES + ONES_ROWS
VMEM_LIMIT = 56 * 1024 * 1024

F32 = jnp.float32
BF16 = jnp.bfloat16


def _rms(x, g):
    return x * lax.rsqrt(jnp.mean(x * x, axis=-1, keepdims=True) + EPS) * g


def _inproj_kernel(x_ref, g_ref, wqt_ref, wk_ref, wvt_ref, wr_ref,
                   zqt_ref, zk_ref, zvt_ref, zr_ref, *, q_scale):
    h = _rms(x_ref[...], g_ref[...]).astype(BF16)
    nt = (((1,), (1,)), ((), ()))
    zqt_ref[...] = (lax.dot_general(wqt_ref[...], h, nt, preferred_element_type=F32)
                    * q_scale).astype(BF16)
    zk_ref[...] = jnp.dot(h, wk_ref[...], preferred_element_type=F32).astype(BF16)
    zvt = lax.dot_general(wvt_ref[...], h, nt, preferred_element_type=F32).astype(BF16)
    ones = jnp.ones((ONES_ROWS, ATT_TK), BF16)
    for c in range(zvt_ref.shape[0]):
        for hd in range(H_A):
            zvt_ref[c, hd * V_ROWS:hd * V_ROWS + LANES, :] = (
                zvt[hd * LANES:(hd + 1) * LANES, c * ATT_TK:(c + 1) * ATT_TK])
            zvt_ref[c, hd * V_ROWS + LANES:(hd + 1) * V_ROWS, :] = ones
    for j in range(wr_ref.shape[1] // COL_TILE):
        cols = slice(j * COL_TILE, (j + 1) * COL_TILE)
        zr_ref[:, cols] = jnp.dot(h, wr_ref[:, cols], preferred_element_type=F32)


def _inproj(x2d, g, wqt, wk, wvt, wr, q_scale):
    n, d = x2d.shape
    att_w = wk.shape[1]
    rec_cols = wr.shape[1]
    const = lambda shape: pl.BlockSpec(shape, lambda i: (0,) * len(shape),
                                       pipeline_mode=pl.Buffered(1))
    return pl.pallas_call(
        functools.partial(_inproj_kernel, q_scale=q_scale),
        grid=(n // ROW_TILE,),
        in_specs=[
            pl.BlockSpec((ROW_TILE, d), lambda i: (i, 0)),
            const((1, d)), const((att_w, d)), const((d, att_w)), const((att_w, d)),
            const((d, rec_cols)),
        ],
        out_specs=[
            pl.BlockSpec((att_w, ROW_TILE), lambda i: (0, i)),
            pl.BlockSpec((ROW_TILE, att_w), lambda i: (i, 0)),
            pl.BlockSpec((ROW_TILE // ATT_TK, H_A * V_ROWS, ATT_TK), lambda i: (i, 0, 0)),
            pl.BlockSpec((ROW_TILE, rec_cols), lambda i: (i, 0)),
        ],
        out_shape=[
            jax.ShapeDtypeStruct((att_w, n), BF16),
            jax.ShapeDtypeStruct((n, att_w), BF16),
            jax.ShapeDtypeStruct((n // ATT_TK, H_A * V_ROWS, ATT_TK), BF16),
            jax.ShapeDtypeStruct((n, rec_cols), F32),
        ],
        compiler_params=pltpu.CompilerParams(
            dimension_semantics=("parallel",), vmem_limit_bytes=VMEM_LIMIT),
        name="inproj",
    )(x2d, g, wqt, wk, wvt, wr)


def _t5_bucket(rel):
    n_half = NUM_BUCKETS // 2
    max_exact = n_half // 2
    ret = jnp.where(rel > 0, n_half, 0)
    n = jnp.abs(rel)
    nf = jnp.maximum(n, 1).astype(jnp.float32)
    large = max_exact + (jnp.log(nf / max_exact) / math.log(MAX_DISTANCE / max_exact)
                         * (n_half - max_exact)).astype(jnp.int32)
    large = jnp.minimum(large, n_half - 1)
    return ret + jnp.where(n < max_exact, n, large)


def _bias_kernel(rb_ref, idx_ref, o_ref):
    h = pl.program_id(0)
    far = rb_ref[NUM_BUCKETS // 2 - 1, h]
    for t in range(2):
        idx = idx_ref[t]
        acc = jnp.zeros(idx.shape, F32)
        for b in range(NUM_BUCKETS):
            acc = jnp.where(idx == b, (rb_ref[b, h] - far) * LOG2E, acc)
        if t == 0:
            krow = lax.broadcasted_iota(jnp.int32, idx.shape, 0)
            qcol = lax.broadcasted_iota(jnp.int32, idx.shape, 1)
            acc = jnp.where(krow // CHUNK <= qcol // CHUNK, acc, NEG_INF)
        o_ref[0, t] = jnp.concatenate([acc, acc], axis=1)


def _bias_tiles(rel_bias):
    assert ATT_TQ == ATT_TK and ATT_TQ % MAX_DISTANCE == 0 and ATT_TQ % CHUNK == 0
    assert MAX_DISTANCE % LANES == 0
    kpos = jnp.arange(ATT_TK)[:, None]
    qpos = jnp.arange(ATT_TQ)[None, :]
    idx = jnp.stack([_t5_bucket(kpos - qpos), _t5_bucket(kpos - ATT_TK - qpos)]).astype(jnp.int32)
    return pl.pallas_call(
        _bias_kernel,
        grid=(H_A,),
        in_specs=[
            pl.BlockSpec(memory_space=pltpu.SMEM),
            pl.BlockSpec((2, ATT_TK, ATT_TQ), lambda h: (0, 0, 0)),
        ],
        out_specs=pl.BlockSpec((1, 2, ATT_TK, 2 * ATT_TQ), lambda h: (h, 0, 0, 0)),
        out_shape=jax.ShapeDtypeStruct((H_A, 2, ATT_TK, 2 * ATT_TQ), F32),
        name="bias_tiles",
    )(rel_bias.astype(F32), idx)


def _attn_kernel(lam_ref, qt_ref, k_ref, vt_ref, b_ref, g_ref, o_ref,
                 qq_ref, m_ref, acc_ref, *, out_scale):
    qi = pl.program_id(1)
    tq = qt_ref.shape[1]
    half = LANES // 2

    zero = jnp.zeros((half, tq), BF16)
    for h in range(H_A):
        qt = qt_ref[h * LANES:(h + 1) * LANES, :]
        qq_ref[h] = jnp.concatenate(
            [jnp.concatenate([qt[:half], zero], axis=0),
             jnp.concatenate([zero, qt[half:]], axis=0)], axis=1)
    m_ref[...] = jnp.full(m_ref.shape, NEG_INF, F32)
    acc_ref[...] = jnp.zeros(acc_ref.shape, F32)

    def scores(j, h):
        start = pl.multiple_of(j * ATT_TK, ATT_TK)
        kj = k_ref[0, pl.ds(start, ATT_TK), h * LANES:(h + 1) * LANES]
        return jnp.dot(kj, qq_ref[h], preferred_element_type=F32)

    def biased(x, h, bias_tile):
        if bias_tile is None:
            return x
        if bias_tile == 0:
            return x + b_ref[h, 0]
        z = ATT_TK - MAX_DISTANCE
        pieces = []
        for c0 in range(0, 2 * tq, MAX_DISTANCE):
            blk = x[z:, c0:c0 + MAX_DISTANCE]
            if c0 % tq == 0:
                blk = blk + b_ref[h, 1, z:, c0:c0 + MAX_DISTANCE]
            pieces.append(blk)
        return jnp.concatenate([x[:z], jnp.concatenate(pieces, axis=1)], axis=0)

    def update(j, h, s, bias_tile):
        m_prev = m_ref[h]
        m_next = jnp.maximum(m_prev, jnp.max(biased(s, h, bias_tile), axis=0, keepdims=True))
        p = jnp.exp2(biased(s - m_next, h, bias_tile)).astype(BF16)
        alpha = jnp.exp2(m_prev - m_next)
        vj = vt_ref[j, h * V_ROWS:(h + 1) * V_ROWS, :]
        acc_ref[h] = alpha * acc_ref[h] + jnp.dot(vj, p, preferred_element_type=F32)
        m_ref[h] = m_next

    def run(blocks):
        items = [(j, h, bt) for j, bt in blocks for h in range(H_A)]
        pending = [scores(j, h) for j, h, _ in items[:QK_AHEAD]]
        for n, (j, h, bt) in enumerate(items):
            if n + QK_AHEAD < len(items):
                jn, hn, _ = items[n + QK_AHEAD]
                pending.append(scores(jn, hn))
            update(j, h, pending.pop(0), bt)

    n_far = jnp.maximum(qi - 1, 0)

    def far_pair(i, carry):
        run([(2 * i, None), (2 * i + 1, None)])
        return carry

    lax.fori_loop(0, n_far // 2, far_pair, 0)

    @pl.when(n_far % 2 == 1)
    def _():
        run([(n_far - 1, None)])

    @pl.when(qi >= 1)
    def _():
        run([(qi - 1, 1), (qi, 0)])

    @pl.when(qi == 0)
    def _():
        run([(qi, 0)])

    for h in range(H_A):
        acc = acc_ref[h]
        ot = acc[:LANES] / acc[LANES:LANES + 1]
        o = jnp.transpose(ot[:, :tq] - lam_ref[0] * ot[:, tq:])
        o_ref[0, :, h * LANES:(h + 1) * LANES] = (
            _rms(o, g_ref[...]) * out_scale).astype(o_ref.dtype)


def _attention(zqt, zk, zvt, bias, lam, g, lam_init, b, s):
    att_w = zk.shape[1]
    nq = s // ATT_TQ
    nk = s // ATT_TK
    kernel = functools.partial(_attn_kernel, out_scale=1.0 - lam_init)
    return pl.pallas_call(
        kernel,
        grid=(b, nq),
        in_specs=[
            pl.BlockSpec(memory_space=pltpu.SMEM),
            pl.BlockSpec((att_w, ATT_TQ), lambda bi, i: (0, bi * nq + i)),
            pl.BlockSpec((1, s, att_w), lambda bi, i: (bi, 0, 0)),
            pl.BlockSpec((nk, H_A * V_ROWS, ATT_TK), lambda bi, i: (bi, 0, 0)),
            pl.BlockSpec((H_A, 2, ATT_TK, 2 * ATT_TQ), lambda bi, i: (0, 0, 0, 0),
                         pipeline_mode=pl.Buffered(1)),
            pl.BlockSpec((1, LANES), lambda bi, i: (0, 0)),
        ],
        out_specs=pl.BlockSpec((1, ATT_TQ, att_w), lambda bi, i: (bi, i, 0)),
        out_shape=jax.ShapeDtypeStruct((b, s, att_w), BF16),
        scratch_shapes=[
            pltpu.VMEM((H_A, LANES, 2 * ATT_TQ), BF16),
            pltpu.VMEM((H_A, 1, 2 * ATT_TQ), F32),
            pltpu.VMEM((H_A, V_ROWS, 2 * ATT_TQ), F32),
        ],
        compiler_params=pltpu.CompilerParams(
            dimension_semantics=("parallel", "arbitrary"), vmem_limit_bytes=VMEM_LIMIT),
        name="diff_attention",
    )(lam, zqt, zk.reshape(b, s, att_w), zvt, bias, g)


def _cumsum_rows(g, tri3):
    hi = g.astype(BF16)
    r1 = g - hi.astype(F32)
    mid = r1.astype(BF16)
    lo = (r1 - mid.astype(F32)).astype(BF16)
    return jnp.dot(tri3, jnp.concatenate([hi, mid, lo], axis=0), preferred_element_type=F32)


def _rec_group(row_slices, rq_ref, rf_ref, ri_ref, rg_ref, llb_ref, l1m_ref, g_ref, o_ref,
               st_ref):
    c = CHUNK
    nt = (((1,), (1,)), ((), ()))
    tn = (((0,), (0,)), ((), ()))
    heads = [slice(h * LANES, (h + 1) * LANES) for h in range(H_R)]
    row = lax.broadcasted_iota(jnp.int32, (c, c), 0)
    col = lax.broadcasted_iota(jnp.int32, (c, c), 1)
    causal = col <= row
    tri = causal.astype(BF16)
    tri3 = jnp.concatenate([tri, tri, tri], axis=1)
    log_lb = llb_ref[...]
    log_1mlb = l1m_ref[...]

    chunks = []
    for rows in row_slices:
        rq = rq_ref[0, rows, :]
        rf2 = rf_ref[0, rows, :] * LOG2E
        ls = jnp.minimum(rf2, 0.0) - jnp.log2(1.0 + jnp.exp2(-jnp.abs(rf2)))
        bterm = log_1mlb + ls
        g = jnp.maximum(log_lb, bterm) + jnp.log2(1.0 + jnp.exp2(-jnp.abs(log_lb - bterm)))
        chunks.append(dict(g=g, kin=jnp.exp2(bterm - rf2),
                           q=rq / (1.0 + jnp.exp2(rq * -LOG2E)),
                           v=ri_ref[0, rows, :].astype(BF16)))
    for d in chunks:
        d["bc"] = _cumsum_rows(d["g"], tri3)

    for d in chunks:
        bc, q, kin = d["bc"], d["q"], d["kin"]
        blast = bc[c - 1:c]
        d["qe"] = (q * jnp.exp2(bc)).astype(BF16)
        d["ke"] = (kin * jnp.exp2(blast - bc)).astype(BF16)
        d["dl"] = jnp.exp2(blast)
        d["lhs"], d["rhs"] = [], []
        for i in range(c // SUB):
            lo, hi = i * SUB, (i + 1) * SUB
            mid = lo + SUB // 2
            u = jnp.clip(bc[lo:hi] - bc[mid:mid + 1], -EXP2_CLAMP, EXP2_CLAMP)
            qd = (q[lo:hi] * jnp.exp2(u)).astype(BF16)
            kd = (kin[lo:hi] * jnp.exp2(-u)).astype(BF16)
            kd_pad = jnp.concatenate(
                ([jnp.zeros((lo, kd.shape[1]), BF16)] if lo else []) + [kd]
                + ([jnp.zeros((c - hi, kd.shape[1]), BF16)] if c - hi else []), axis=0)
            if i == 0:
                d["lhs"].append((qd,))
                d["rhs"].append((kd_pad,))
            else:
                ref = bc[lo:lo + 1]
                qo = (q[lo:hi] * jnp.exp2(bc[lo:hi] - ref)).astype(BF16)
                ko = (kin[:lo] * jnp.exp2(ref - bc[:lo])).astype(BF16)
                ko_pad = jnp.concatenate([ko, jnp.zeros((c - lo, ko.shape[1]), BF16)], axis=0)
                d["lhs"].append((qo, qd))
                d["rhs"].append((ko_pad, kd_pad))

    for d in chunks:
        d["inc"] = [lax.dot_general(d["v"][:, hs], d["ke"][:, hs], tn, preferred_element_type=F32)
                    for hs in heads]
    for d in chunks:
        d["a"] = []
        for hs in heads:
            a_rows = [lax.dot_general(jnp.concatenate([p[:, hs] for p in lhs], axis=1),
                                      jnp.concatenate([p[:, hs] for p in rhs], axis=1),
                                      nt, preferred_element_type=F32)
                      for lhs, rhs in zip(d["lhs"], d["rhs"])]
            d["a"].append(jnp.where(causal, jnp.concatenate(a_rows, axis=0), 0.0).astype(BF16))
    states = [st_ref[h] for h in range(H_R)]
    for d in chunks:
        d["inter"] = [lax.dot_general(d["qe"][:, hs], states[h].astype(BF16), nt,
                                      preferred_element_type=F32)
                      for h, hs in enumerate(heads)]
        states = [states[h] * d["dl"][:, hs] + d["inc"][h] for h, hs in enumerate(heads)]
    for h in range(H_R):
        st_ref[h] = states[h]
    for rows, d in zip(row_slices, chunks):
        rg = rg_ref[0, rows, :]
        gate = rg / (1.0 + jnp.exp2(rg * -LOG2E))
        for h, hs in enumerate(heads):
            o = d["inter"][h] + jnp.dot(d["a"][h], d["v"][:, hs], preferred_element_type=F32)
            o_ref[0, rows, hs] = (_rms(o, g_ref[...]) * gate[:, hs]).astype(o_ref.dtype)


def _rec_kernel(rq_ref, rf_ref, ri_ref, rg_ref, llb_ref, l1m_ref, g_ref, o_ref, st_ref):
    @pl.when(pl.program_id(1) == 0)
    def _():
        st_ref[...] = jnp.zeros(st_ref.shape, F32)

    group = REC_GROUP * CHUNK

    def body(gi, carry):
        r0 = pl.multiple_of(gi * group, group)
        _rec_group([pl.ds(r0 + n * CHUNK, CHUNK) for n in range(REC_GROUP)],
                   rq_ref, rf_ref, ri_ref, rg_ref, llb_ref, l1m_ref, g_ref, o_ref, st_ref)
        return carry

    lax.fori_loop(0, rq_ref.shape[1] // group, body, 0)


def _recurrence(zr, log_lb, log_1mlb, g):
    b, s, cols4 = zr.shape
    w = cols4 // 4
    spec = lambda j: pl.BlockSpec((1, REC_TS, w), lambda bi, si, j=j: (bi, si, j))
    vec = lambda n: pl.BlockSpec((1, n), lambda bi, si: (0, 0))
    return pl.pallas_call(
        _rec_kernel,
        grid=(b, s // REC_TS),
        in_specs=[spec(0), spec(1), spec(2), spec(3), vec(w), vec(w), vec(LANES)],
        out_specs=pl.BlockSpec((1, REC_TS, w), lambda bi, si: (bi, si, 0)),
        out_shape=jax.ShapeDtypeStruct((b, s, w), BF16),
        scratch_shapes=[pltpu.VMEM((H_R, LANES, LANES), F32)],
        compiler_params=pltpu.CompilerParams(
            dimension_semantics=("parallel", "arbitrary"), vmem_limit_bytes=VMEM_LIMIT),
        name="hgrn2",
    )(zr, zr, zr, zr, log_lb, log_1mlb, g)


def _mlp_kernel(x_ref, oa_ref, or_ref, wo_ref, g2_ref, wu_ref, wd_ref, gf_ref, o_ref, u2_ref,
                *, final_norm):
    wa = oa_ref.shape[1]
    x = x_ref[...]
    x = x + jnp.dot(oa_ref[...], wo_ref[:wa, :], preferred_element_type=F32)
    x = x + jnp.dot(or_ref[...], wo_ref[wa:, :], preferred_element_type=F32)
    h2 = _rms(x, g2_ref[...]).astype(BF16)
    d_ff = wu_ref.shape[1]
    for f in range(d_ff // COL_TILE):
        cols = slice(f * COL_TILE, (f + 1) * COL_TILE)
        u = jnp.maximum(jnp.dot(h2, wu_ref[:, cols], preferred_element_type=F32), 0.0)
        u2_ref[:, cols] = (u * u).astype(BF16)
    x = x + jnp.dot(u2_ref[...], wd_ref[...], preferred_element_type=F32)
    if final_norm:
        x = _rms(x, gf_ref[...])
    o_ref[...] = x


def _mlp(x2d, oa, orr, wo, g2, wu, wd, gf, final_norm):
    n, d = x2d.shape
    wa, wr = oa.shape[1], orr.shape[1]
    d_ff = wu.shape[1]
    const = lambda shape: pl.BlockSpec(shape, lambda i: (0, 0), pipeline_mode=pl.Buffered(1))
    return pl.pallas_call(
        functools.partial(_mlp_kernel, final_norm=final_norm),
        grid=(n // ROW_TILE,),
        in_specs=[
            pl.BlockSpec((ROW_TILE, d), lambda i: (i, 0)),
            pl.BlockSpec((ROW_TILE, wa), lambda i: (i, 0)),
            pl.BlockSpec((ROW_TILE, wr), lambda i: (i, 0)),
            const((wa + wr, d)),
            const((1, d)),
            const((d, d_ff)),
            const((d_ff, d)),
            const((1, d)),
        ],
        out_specs=pl.BlockSpec((ROW_TILE, d), lambda i: (i, 0)),
        out_shape=jax.ShapeDtypeStruct((n, d), F32),
        scratch_shapes=[pltpu.VMEM((ROW_TILE, d_ff), BF16)],
        compiler_params=pltpu.CompilerParams(
            dimension_semantics=("parallel",), vmem_limit_bytes=VMEM_LIMIT),
        name="outproj_mlp",
    )(x2d, oa, orr, wo, g2, wu, wd, gf)


def kernel(x, norm1_g, w_in, lam_qk, attn_norm_g, lb_logits, hgrn_norm_g, w_out, norm2_g,
           w_up, w_down, rel_bias, final_g):
    b, s, d = x.shape
    depth = w_in.shape[0]
    att_w = attn_norm_g.shape[1] * H_A
    rec_w = hgrn_norm_g.shape[1] * H_R
    assert attn_norm_g.shape[1] == LANES and hgrn_norm_g.shape[1] == LANES
    assert w_in.shape[2] == 3 * att_w + 4 * rec_w
    assert (b * s) % ROW_TILE == 0 and s % ATT_TQ == 0 and s % REC_TS == 0

    lb = jnp.cumsum(jax.nn.softmax(lb_logits.astype(F32), axis=0), axis=0)
    lb = lb - lb[0:1]
    log_lb = jnp.log(lb) * LOG2E
    log_1mlb = jnp.log1p(-lb) * LOG2E

    bias = _bias_tiles(rel_bias)
    x2d = x.reshape(b * s, d)
    q_scale = (att_w // (2 * H_A)) ** -0.5 * LOG2E
    for l in range(depth):
        w = w_in[l].astype(BF16)
        zqt, zk, zvt, zr = _inproj(
            x2d, norm1_g[l][None], w[:, :att_w].T, w[:, att_w:2 * att_w],
            w[:, 2 * att_w:3 * att_w].T, w[:, 3 * att_w:], q_scale)

        lam_init = 0.8 - 0.6 * math.exp(-0.3 * l)
        lq = lam_qk[l].astype(F32)
        lam = jnp.exp(jnp.sum(lq[0] * lq[1])) - jnp.exp(jnp.sum(lq[2] * lq[3])) + lam_init
        oa = _attention(zqt, zk, zvt, bias, lam.reshape(1), attn_norm_g[l][None].astype(F32),
                        lam_init, b, s)
        orr = _recurrence(zr.reshape(b, s, 4 * rec_w), log_lb[l][None], log_1mlb[l][None],
                          hgrn_norm_g[l][None].astype(F32))
        x2d = _mlp(x2d, oa.reshape(b * s, att_w), orr.reshape(b * s, rec_w),
                   w_out[l].astype(BF16), norm2_g[l][None], w_up[l].astype(BF16),
                   w_down[l].astype(BF16), final_g[None], l == depth - 1)
    return x2d.reshape(b, s, d)
```

```python
import functools
import math

import jax
import jax.numpy as jnp
from jax import lax
from jax.experimental import pallas as pl
from jax.experimental.pallas import tpu as pltpu

CHUNK = 64
H_A = 4
H_R = 4
NUM_BUCKETS = 32
MAX_DISTANCE = 128
EPS = 1e-6
NEG_INF = -1e30

LANES = 128
ROW_TILE = 1024
COL_TILE = 512
ATT_TQ = 256
ATT_TK = 256
QK_AHEAD = 2
REC_TS = 1024
REC_GROUP = 8
SUB = 16
LOG2E = 1.4426950408889634
EXP2_CLAMP = 80.0 * LOG2E
ONES_ROWS = 16
V_ROWS = LANES + ONES_ROWS
VMEM_LIMIT = 56 * 1024 * 1024

F32 = jnp.float32
BF16 = jnp.bfloat16


def _rms(x, g):
    return x * lax.rsqrt(jnp.mean(x * x, axis=-1, keepdims=True) + EPS) * g


def _inproj_kernel(x_ref, g_ref, wqt_ref, wk_ref, wvt_ref, wr_ref,
                   zqt_ref, zk_ref, zvt_ref, zr_ref, *, q_scale):
    h = _rms(x_ref[...], g_ref[...]).astype(BF16)
    nt = (((1,), (1,)), ((), ()))
    zqt_ref[...] = (lax.dot_general(wqt_ref[...], h, nt, preferred_element_type=F32)
                    * q_scale).astype(BF16)
    zk_ref[...] = jnp.dot(h, wk_ref[...], preferred_element_type=F32).astype(BF16)
    zvt = lax.dot_general(wvt_ref[...], h, nt, preferred_element_type=F32).astype(BF16)
    ones = jnp.ones((ONES_ROWS, ATT_TK), BF16)
    for c in range(zvt_ref.shape[0]):
        for hd in range(H_A):
            zvt_ref[c, hd * V_ROWS:hd * V_ROWS + LANES, :] = (
                zvt[hd * LANES:(hd + 1) * LANES, c * ATT_TK:(c + 1) * ATT_TK])
            zvt_ref[c, hd * V_ROWS + LANES:(hd + 1) * V_ROWS, :] = ones
    for j in range(wr_ref.shape[1] // COL_TILE):
        cols = slice(j * COL_TILE, (j + 1) * COL_TILE)
        zr_ref[:, cols] = jnp.dot(h, wr_ref[:, cols], preferred_element_type=F32)


def _inproj(x2d, g, wqt, wk, wvt, wr, q_scale):
    n, d = x2d.shape
    att_w = wk.shape[1]
    rec_cols = wr.shape[1]
    const = lambda shape: pl.BlockSpec(shape, lambda i: (0,) * len(shape),
                                       pipeline_mode=pl.Buffered(1))
    return pl.pallas_call(
        functools.partial(_inproj_kernel, q_scale=q_scale),
        grid=(n // ROW_TILE,),
        in_specs=[
            pl.BlockSpec((ROW_TILE, d), lambda i: (i, 0)),
            const((1, d)), const((att_w, d)), const((d, att_w)), const((att_w, d)),
            const((d, rec_cols)),
        ],
        out_specs=[
            pl.BlockSpec((att_w, ROW_TILE), lambda i: (0, i)),
            pl.BlockSpec((ROW_TILE, att_w), lambda i: (i, 0)),
            pl.BlockSpec((ROW_TILE // ATT_TK, H_A * V_ROWS, ATT_TK), lambda i: (i, 0, 0)),
            pl.BlockSpec((ROW_TILE, rec_cols), lambda i: (i, 0)),
        ],
        out_shape=[
            jax.ShapeDtypeStruct((att_w, n), BF16),
            jax.ShapeDtypeStruct((n, att_w), BF16),
            jax.ShapeDtypeStruct((n // ATT_TK, H_A * V_ROWS, ATT_TK), BF16),
            jax.ShapeDtypeStruct((n, rec_cols), F32),
        ],
        compiler_params=pltpu.CompilerParams(
            dimension_semantics=("parallel",), vmem_limit_bytes=VMEM_LIMIT),
        name="inproj",
    )(x2d, g, wqt, wk, wvt, wr)


def _t5_bucket(rel):
    n_half = NUM_BUCKETS // 2
    max_exact = n_half // 2
    ret = jnp.where(rel > 0, n_half, 0)
    n = jnp.abs(rel)
    nf = jnp.maximum(n, 1).astype(jnp.float32)
    large = max_exact + (jnp.log(nf / max_exact) / math.log(MAX_DISTANCE / max_exact)
                         * (n_half - max_exact)).astype(jnp.int32)
    large = jnp.minimum(large, n_half - 1)
    return ret + jnp.where(n < max_exact, n, large)


def _bias_kernel(rb_ref, idx_ref, o_ref):
    h = pl.program_id(0)
    far = rb_ref[NUM_BUCKETS // 2 - 1, h]
    for t in range(2):
        idx = idx_ref[t]
        acc = jnp.zeros(idx.shape, F32)
        for b in range(NUM_BUCKETS):
            acc = jnp.where(idx == b, (rb_ref[b, h] - far) * LOG2E, acc)
        if t == 0:
            krow = lax.broadcasted_iota(jnp.int32, idx.shape, 0)
            qcol = lax.broadcasted_iota(jnp.int32, idx.shape, 1)
            acc = jnp.where(krow // CHUNK <= qcol // CHUNK, acc, NEG_INF)
        o_ref[0, t] = jnp.concatenate([acc, acc], axis=1)


def _bias_tiles(rel_bias):
    assert ATT_TQ == ATT_TK and ATT_TQ % MAX_DISTANCE == 0 and ATT_TQ % CHUNK == 0
    assert MAX_DISTANCE % LANES == 0
    kpos = jnp.arange(ATT_TK)[:, None]
    qpos = jnp.arange(ATT_TQ)[None, :]
    idx = jnp.stack([_t5_bucket(kpos - qpos), _t5_bucket(kpos - ATT_TK - qpos)]).astype(jnp.int32)
    return pl.pallas_call(
        _bias_kernel,
        grid=(H_A,),
        in_specs=[
            pl.BlockSpec(memory_space=pltpu.SMEM),
            pl.BlockSpec((2, ATT_TK, ATT_TQ), lambda h: (0, 0, 0)),
        ],
        out_specs=pl.BlockSpec((1, 2, ATT_TK, 2 * ATT_TQ), lambda h: (h, 0, 0, 0)),
        out_shape=jax.ShapeDtypeStruct((H_A, 2, ATT_TK, 2 * ATT_TQ), F32),
        name="bias_tiles",
    )(rel_bias.astype(F32), idx)


def _attn_kernel(lam_ref, qt_ref, k_ref, vt_ref, b_ref, g_ref, o_ref,
                 qq_ref, m_ref, acc_ref, *, out_scale):
    qi = pl.program_id(1)
    tq = qt_ref.shape[1]
    half = LANES // 2

    zero = jnp.zeros((half, tq), BF16)
    for h in range(H_A):
        qt = qt_ref[h * LANES:(h + 1) * LANES, :]
        qq_ref[h] = jnp.concatenate(
            [jnp.concatenate([qt[:half], zero], axis=0),
             jnp.concatenate([zero, qt[half:]], axis=0)], axis=1)
    m_ref[...] = jnp.full(m_ref.shape, NEG_INF, F32)
    acc_ref[...] = jnp.zeros(acc_ref.shape, F32)

    def scores(j, h):
        start = pl.multiple_of(j * ATT_TK, ATT_TK)
        kj = k_ref[0, pl.ds(start, ATT_TK), h * LANES:(h + 1) * LANES]
        return jnp.dot(kj, qq_ref[h], preferred_element_type=F32)

    def biased(x, h, bias_tile):
        if bias_tile is None:
            return x
        if bias_tile == 0:
            return x + b_ref[h, 0]
        z = ATT_TK - MAX_DISTANCE
        pieces = []
        for c0 in range(0, 2 * tq, MAX_DISTANCE):
            blk = x[z:, c0:c0 + MAX_DISTANCE]
            if c0 % tq == 0:
                blk = blk + b_ref[h, 1, z:, c0:c0 + MAX_DISTANCE]
            pieces.append(blk)
        return jnp.concatenate([x[:z], jnp.concatenate(pieces, axis=1)], axis=0)

    def update(j, h, s, bias_tile):
        m_prev = m_ref[h]
        m_next = jnp.maximum(m_prev, jnp.max(biased(s, h, bias_tile), axis=0, keepdims=True))
        p = jnp.exp2(biased(s - m_next, h, bias_tile)).astype(BF16)
        alpha = jnp.exp2(m_prev - m_next)
        vj = vt_ref[j, h * V_ROWS:(h + 1) * V_ROWS, :]
        acc_ref[h] = alpha * acc_ref[h] + jnp.dot(vj, p, preferred_element_type=F32)
        m_ref[h] = m_next

    def run(blocks):
        items = [(j, h, bt) for j, bt in blocks for h in range(H_A)]
        pending = [scores(j, h) for j, h, _ in items[:QK_AHEAD]]
        for n, (j, h, bt) in enumerate(items):
            if n + QK_AHEAD < len(items):
                jn, hn, _ = items[n + QK_AHEAD]
                pending.append(scores(jn, hn))
            update(j, h, pending.pop(0), bt)

    n_far = jnp.maximum(qi - 1, 0)

    def far_pair(i, carry):
        run([(2 * i, None), (2 * i + 1, None)])
        return carry

    lax.fori_loop(0, n_far // 2, far_pair, 0)

    @pl.when(n_far % 2 == 1)
    def _():
        run([(n_far - 1, None)])

    @pl.when(qi >= 1)
    def _():
        run([(qi - 1, 1), (qi, 0)])

    @pl.when(qi == 0)
    def _():
        run([(qi, 0)])

    for h in range(H_A):
        acc = acc_ref[h]
        ot = acc[:LANES] / acc[LANES:LANES + 1]
        o = jnp.transpose(ot[:, :tq] - lam_ref[0] * ot[:, tq:])
        o_ref[0, :, h * LANES:(h + 1) * LANES] = (
            _rms(o, g_ref[...]) * out_scale).astype(o_ref.dtype)


def _attention(zqt, zk, zvt, bias, lam, g, lam_init, b, s):
    att_w = zk.shape[1]
    nq = s // ATT_TQ
    nk = s // ATT_TK
    kernel = functools.partial(_attn_kernel, out_scale=1.0 - lam_init)
    return pl.pallas_call(
        kernel,
        grid=(b, nq),
        in_specs=[
            pl.BlockSpec(memory_space=pltpu.SMEM),
            pl.BlockSpec((att_w, ATT_TQ), lambda bi, i: (0, bi * nq + i)),
            pl.BlockSpec((1, s, att_w), lambda bi, i: (bi, 0, 0)),
            pl.BlockSpec((nk, H_A * V_ROWS, ATT_TK), lambda bi, i: (bi, 0, 0)),
            pl.BlockSpec((H_A, 2, ATT_TK, 2 * ATT_TQ), lambda bi, i: (0, 0, 0, 0),
                         pipeline_mode=pl.Buffered(1)),
            pl.BlockSpec((1, LANES), lambda bi, i: (0, 0)),
        ],
        out_specs=pl.BlockSpec((1, ATT_TQ, att_w), lambda bi, i: (bi, i, 0)),
        out_shape=jax.ShapeDtypeStruct((b, s, att_w), BF16),
        scratch_shapes=[
            pltpu.VMEM((H_A, LANES, 2 * ATT_TQ), BF16),
            pltpu.VMEM((H_A, 1, 2 * ATT_TQ), F32),
            pltpu.VMEM((H_A, V_ROWS, 2 * ATT_TQ), F32),
        ],
        compiler_params=pltpu.CompilerParams(
            dimension_semantics=("parallel", "arbitrary"), vmem_limit_bytes=VMEM_LIMIT),
        name="diff_attention",
    )(lam, zqt, zk.reshape(b, s, att_w), zvt, bias, g)


def _cumsum_rows(g, tri3):
    hi = g.astype(BF16)
    r1 = g - hi.astype(F32)
    mid = r1.astype(BF16)
    lo = (r1 - mid.astype(F32)).astype(BF16)
    return jnp.dot(tri3, jnp.concatenate([hi, mid, lo], axis=0), preferred_element_type=F32)


def _rec_group(row_slices, rq_ref, rf_ref, ri_ref, rg_ref, llb_ref, l1m_ref, g_ref, o_ref,
               st_ref):
    c = CHUNK
    nt = (((1,), (1,)), ((), ()))
    tn = (((0,), (0,)), ((), ()))
    heads = [slice(h * LANES, (h + 1) * LANES) for h in range(H_R)]
    row = lax.broadcasted_iota(jnp.int32, (c, c), 0)
    col = lax.broadcasted_iota(jnp.int32, (c, c), 1)
    causal = col <= row
    tri = causal.astype(BF16)
    tri3 = jnp.concatenate([tri, tri, tri], axis=1)
    log_lb = llb_ref[...]
    log_1mlb = l1m_ref[...]

    chunks = []
    for rows in row_slices:
        rq = rq_ref[0, rows, :]
        rf2 = rf_ref[0, rows, :] * LOG2E
        ls = jnp.minimum(rf2, 0.0) - jnp.log2(1.0 + jnp.exp2(-jnp.abs(rf2)))
        bterm = log_1mlb + ls
        g = jnp.maximum(log_lb, bterm) + jnp.log2(1.0 + jnp.exp2(-jnp.abs(log_lb - bterm)))
        chunks.append(dict(g=g, kin=jnp.exp2(bterm - rf2),
                           q=rq / (1.0 + jnp.exp2(rq * -LOG2E)),
                           v=ri_ref[0, rows, :].astype(BF16)))
    for d in chunks:
        d["bc"] = _cumsum_rows(d["g"], tri3)

    for d in chunks:
        bc, q, kin = d["bc"], d["q"], d["kin"]
        blast = bc[c - 1:c]
        d["qe"] = (q * jnp.exp2(bc)).astype(BF16)
        d["ke"] = (kin * jnp.exp2(blast - bc)).astype(BF16)
        d["dl"] = jnp.exp2(blast)
        d["lhs"], d["rhs"] = [], []
        for i in range(c // SUB):
            lo, hi = i * SUB, (i + 1) * SUB
            mid = lo + SUB // 2
            u = jnp.clip(bc[lo:hi] - bc[mid:mid + 1], -EXP2_CLAMP, EXP2_CLAMP)
            qd = (q[lo:hi] * jnp.exp2(u)).astype(BF16)
            kd = (kin[lo:hi] * jnp.exp2(-u)).astype(BF16)
            kd_pad = jnp.concatenate(
                ([jnp.zeros((lo, kd.shape[1]), BF16)] if lo else []) + [kd]
                + ([jnp.zeros((c - hi, kd.shape[1]), BF16)] if c - hi else []), axis=0)
            if i == 0:
                d["lhs"].append((qd,))
                d["rhs"].append((kd_pad,))
            else:
                ref = bc[lo:lo + 1]
                qo = (q[lo:hi] * jnp.exp2(bc[lo:hi] - ref)).astype(BF16)
                ko = (kin[:lo] * jnp.exp2(ref - bc[:lo])).astype(BF16)
                ko_pad = jnp.concatenate([ko, jnp.zeros((c - lo, ko.shape[1]), BF16)], axis=0)
                d["lhs"].append((qo, qd))
                d["rhs"].append((ko_pad, kd_pad))

    for d in chunks:
        d["inc"] = [lax.dot_general(d["v"][:, hs], d["ke"][:, hs], tn, preferred_element_type=F32)
                    for hs in heads]
    for d in chunks:
        d["a"] = []
        for hs in heads:
            a_rows = [lax.dot_general(jnp.concatenate([p[:, hs] for p in lhs], axis=1),
                                      jnp.concatenate([p[:, hs] for p in rhs], axis=1),
                                      nt, preferred_element_type=F32)
                      for lhs, rhs in zip(d["lhs"], d["rhs"])]
            d["a"].append(jnp.where(causal, jnp.concatenate(a_rows, axis=0), 0.0).astype(BF16))
    states = [st_ref[h] for h in range(H_R)]
    for d in chunks:
        d["inter"] = [lax.dot_general(d["qe"][:, hs], states[h].astype(BF16), nt,
                                      preferred_element_type=F32)
                      for h, hs in enumerate(heads)]
        states = [states[h] * d["dl"][:, hs] + d["inc"][h] for h, hs in enumerate(heads)]
    for h in range(H_R):
        st_ref[h] = states[h]
    for rows, d in zip(row_slices, chunks):
        rg = rg_ref[0, rows, :]
        gate = rg / (1.0 + jnp.exp2(rg * -LOG2E))
        for h, hs in enumerate(heads):
            o = d["inter"][h] + jnp.dot(d["a"][h], d["v"][:, hs], preferred_element_type=F32)
            o_ref[0, rows, hs] = (_rms(o, g_ref[...]) * gate[:, hs]).astype(o_ref.dtype)


def _rec_kernel(rq_ref, rf_ref, ri_ref, rg_ref, llb_ref, l1m_ref, g_ref, o_ref, st_ref):
    @pl.when(pl.program_id(1) == 0)
    def _():
        st_ref[...] = jnp.zeros(st_ref.shape, F32)

    group = REC_GROUP * CHUNK

    def body(gi, carry):
        r0 = pl.multiple_of(gi * group, group)
        _rec_group([pl.ds(r0 + n * CHUNK, CHUNK) for n in range(REC_GROUP)],
                   rq_ref, rf_ref, ri_ref, rg_ref, llb_ref, l1m_ref, g_ref, o_ref, st_ref)
        return carry

    lax.fori_loop(0, rq_ref.shape[1] // group, body, 0)


def _recurrence(zr, log_lb, log_1mlb, g):
    b, s, cols4 = zr.shape
    w = cols4 // 4
    spec = lambda j: pl.BlockSpec((1, REC_TS, w), lambda bi, si, j=j: (bi, si, j))
    vec = lambda n: pl.BlockSpec((1, n), lambda bi, si: (0, 0))
    return pl.pallas_call(
        _rec_kernel,
        grid=(b, s // REC_TS),
        in_specs=[spec(0), spec(1), spec(2), spec(3), vec(w), vec(w), vec(LANES)],
        out_specs=pl.BlockSpec((1, REC_TS, w), lambda bi, si: (bi, si, 0)),
        out_shape=jax.ShapeDtypeStruct((b, s, w), BF16),
        scratch_shapes=[pltpu.VMEM((H_R, LANES, LANES), F32)],
        compiler_params=pltpu.CompilerParams(
            dimension_semantics=("parallel", "arbitrary"), vmem_limit_bytes=VMEM_LIMIT),
        name="hgrn2",
    )(zr, zr, zr, zr, log_lb, log_1mlb, g)


def _mlp_kernel(x_ref, oa_ref, or_ref, wo_ref, g2_ref, wu_ref, wd_ref, gf_ref, o_ref, u2_ref,
                *, final_norm):
    wa = oa_ref.shape[1]
    x = x_ref[...]
    x = x + jnp.dot(oa_ref[...], wo_ref[:wa, :], preferred_element_type=F32)
    x = x + jnp.dot(or_ref[...], wo_ref[wa:, :], preferred_element_type=F32)
    h2 = _rms(x, g2_ref[...]).astype(BF16)
    d_ff = wu_ref.shape[1]
    for f in range(d_ff // COL_TILE):
        cols = slice(f * COL_TILE, (f + 1) * COL_TILE)
        u = jnp.maximum(jnp.dot(h2, wu_ref[:, cols], preferred_element_type=F32), 0.0)
        u2_ref[:, cols] = (u * u).astype(BF16)
    x = x + jnp.dot(u2_ref[...], wd_ref[...], preferred_element_type=F32)
    if final_norm:
        x = _rms(x, gf_ref[...])
    o_ref[...] = x


def _mlp(x2d, oa, orr, wo, g2, wu, wd, gf, final_norm):
    n, d = x2d.shape
    wa, wr = oa.shape[1], orr.shape[1]
    d_ff = wu.shape[1]
    const = lambda shape: pl.BlockSpec(shape, lambda i: (0, 0), pipeline_mode=pl.Buffered(1))
    return pl.pallas_call(
        functools.partial(_mlp_kernel, final_norm=final_norm),
        grid=(n // ROW_TILE,),
        in_specs=[
            pl.BlockSpec((ROW_TILE, d), lambda i: (i, 0)),
            pl.BlockSpec((ROW_TILE, wa), lambda i: (i, 0)),
            pl.BlockSpec((ROW_TILE, wr), lambda i: (i, 0)),
            const((wa + wr, d)),
            const((1, d)),
            const((d, d_ff)),
            const((d_ff, d)),
            const((1, d)),
        ],
        out_specs=pl.BlockSpec((ROW_TILE, d), lambda i: (i, 0)),
        out_shape=jax.ShapeDtypeStruct((n, d), F32),
        scratch_shapes=[pltpu.VMEM((ROW_TILE, d_ff), BF16)],
        compiler_params=pltpu.CompilerParams(
            dimension_semantics=("parallel",), vmem_limit_bytes=VMEM_LIMIT),
        name="outproj_mlp",
    )(x2d, oa, orr, wo, g2, wu, wd, gf)


def kernel(x, norm1_g, w_in, lam_qk, attn_norm_g, lb_logits, hgrn_norm_g, w_out, norm2_g,
           w_up, w_down, rel_bias, final_g):
    b, s, d = x.shape
    depth = w_in.shape[0]
    att_w = attn_norm_g.shape[1] * H_A
    rec_w = hgrn_norm_g.shape[1] * H_R
    assert attn_norm_g.shape[1] == LANES and hgrn_norm_g.shape[1] == LANES
    assert w_in.shape[2] == 3 * att_w + 4 * rec_w
    assert (b * s) % ROW_TILE == 0 and s % ATT_TQ == 0 and s % REC_TS == 0

    lb = jnp.cumsum(jax.nn.softmax(lb_logits.astype(F32), axis=0), axis=0)
    lb = lb - lb[0:1]
    log_lb = jnp.log(lb) * LOG2E
    log_1mlb = jnp.log1p(-lb) * LOG2E

    bias = _bias_tiles(rel_bias)
    x2d = x.reshape(b * s, d)
    q_scale = (att_w // (2 * H_A)) ** -0.5 * LOG2E
    for l in range(depth):
        w = w_in[l].astype(BF16)
        zqt, zk, zvt, zr = _inproj(
            x2d, norm1_g[l][None], w[:, :att_w].T, w[:, att_w:2 * att_w],
            w[:, 2 * att_w:3 * att_w].T, w[:, 3 * att_w:], q_scale)

        lam_init = 0.8 - 0.6 * math.exp(-0.3 * l)
        lq = lam_qk[l].astype(F32)
        lam = jnp.exp(jnp.sum(lq[0] * lq[1])) - jnp.exp(jnp.sum(lq[2] * lq[3])) + lam_init
        oa = _attention(zqt, zk, zvt, bias, lam.reshape(1), attn_norm_g[l][None].astype(F32),
                        lam_init, b, s)
        orr = _recurrence(zr.reshape(b, s, 4 * rec_w), log_lb[l][None], log_1mlb[l][None],
                          hgrn_norm_g[l][None].astype(F32))
        x2d = _mlp(x2d, oa.reshape(b * s, att_w), orr.reshape(b * s, rec_w),
                   w_out[l].astype(BF16), norm2_g[l][None], w_up[l].astype(BF16),
                   w_down[l].astype(BF16), final_g[None], l == depth - 1)
    return x2d.reshape(b, s, d)
```

```python
import functools
import math

import jax
import jax.numpy as jnp
from jax import lax
from jax.experimental import pallas as pl
from jax.experimental.pallas import tpu as pltpu

CHUNK = 64
H_A = 4
H_R = 4
NUM_BUCKETS = 32
MAX_DISTANCE = 128
EPS = 1e-6
NEG_INF = -1e30

LANES = 128
ROW_TILE = 1024
COL_TILE = 512
ATT_TQ = 256
ATT_TK = 256
QK_AHEAD = 2
REC_TS = 1024
REC_GROUP = 8
SUB = 16
LOG2E = 1.4426950408889634
EXP2_CLAMP = 80.0 * LOG2E
ONES_ROWS = 16
V_ROWS = LANES + ONES_ROWS
VMEM_LIMIT = 56 * 1024 * 1024

F32 = jnp.float32
BF16 = jnp.bfloat16


def _rms(x, g):
    return x * lax.rsqrt(jnp.mean(x * x, axis=-1, keepdims=True) + EPS) * g


def _inproj_kernel(x_ref, g_ref, wqt_ref, wk_ref, wvt_ref, wr_ref,
                   zqt_ref, zk_ref, zvt_ref, zr_ref, *, q_scale):
    h = _rms(x_ref[...], g_ref[...]).astype(BF16)
    nt = (((1,), (1,)), ((), ()))
    zqt_ref[...] = (lax.dot_general(wqt_ref[...], h, nt, preferred_element_type=F32)
                    * q_scale).astype(BF16)
    zk_ref[...] = jnp.dot(h, wk_ref[...], preferred_element_type=F32).astype(BF16)
    zvt = lax.dot_general(wvt_ref[...], h, nt, preferred_element_type=F32).astype(BF16)
    ones = jnp.ones((ONES_ROWS, ATT_TK), BF16)
    for c in range(zvt_ref.shape[0]):
        for hd in range(H_A):
            zvt_ref[c, hd * V_ROWS:hd * V_ROWS + LANES, :] = (
                zvt[hd * LANES:(hd + 1) * LANES, c * ATT_TK:(c + 1) * ATT_TK])
            zvt_ref[c, hd * V_ROWS + LANES:(hd + 1) * V_ROWS, :] = ones
    for j in range(wr_ref.shape[1] // COL_TILE):
        cols = slice(j * COL_TILE, (j + 1) * COL_TILE)
        zr_ref[:, cols] = jnp.dot(h, wr_ref[:, cols], preferred_element_type=F32)


def _inproj(x2d, g, wqt, wk, wvt, wr, q_scale):
    n, d = x2d.shape
    att_w = wk.shape[1]
    rec_cols = wr.shape[1]
    const = lambda shape: pl.BlockSpec(shape, lambda i: (0,) * len(shape),
                                       pipeline_mode=pl.Buffered(1))
    return pl.pallas_call(
        functools.partial(_inproj_kernel, q_scale=q_scale),
        grid=(n // ROW_TILE,),
        in_specs=[
            pl.BlockSpec((ROW_TILE, d), lambda i: (i, 0)),
            const((1, d)), const((att_w, d)), const((d, att_w)), const((att_w, d)),
            const((d, rec_cols)),
        ],
        out_specs=[
            pl.BlockSpec((att_w, ROW_TILE), lambda i: (0, i)),
            pl.BlockSpec((ROW_TILE, att_w), lambda i: (i, 0)),
            pl.BlockSpec((ROW_TILE // ATT_TK, H_A * V_ROWS, ATT_TK), lambda i: (i, 0, 0)),
            pl.BlockSpec((ROW_TILE, rec_cols), lambda i: (i, 0)),
        ],
        out_shape=[
            jax.ShapeDtypeStruct((att_w, n), BF16),
            jax.ShapeDtypeStruct((n, att_w), BF16),
            jax.ShapeDtypeStruct((n // ATT_TK, H_A * V_ROWS, ATT_TK), BF16),
            jax.ShapeDtypeStruct((n, rec_cols), F32),
        ],
        compiler_params=pltpu.CompilerParams(
            dimension_semantics=("parallel",), vmem_limit_bytes=VMEM_LIMIT),
        name="inproj",
    )(x2d, g, wqt, wk, wvt, wr)


def _t5_bucket(rel):
    n_half = NUM_BUCKETS // 2
    max_exact = n_half // 2
    ret = jnp.where(rel > 0, n_half, 0)
    n = jnp.abs(rel)
    nf = jnp.maximum(n, 1).astype(jnp.float32)
    large = max_exact + (jnp.log(nf / max_exact) / math.log(MAX_DISTANCE / max_exact)
                         * (n_half - max_exact)).astype(jnp.int32)
    large = jnp.minimum(large, n_half - 1)
    return ret + jnp.where(n < max_exact, n, large)


def _bias_kernel(rb_ref, idx_ref, o_ref):
    h = pl.program_id(0)
    far = rb_ref[NUM_BUCKETS // 2 - 1, h]
    for t in range(2):
        idx = idx_ref[t]
        acc = jnp.zeros(idx.shape, F32)
        for b in range(NUM_BUCKETS):
            acc = jnp.where(idx == b, (rb_ref[b, h] - far) * LOG2E, acc)
        if t == 0:
            krow = lax.broadcasted_iota(jnp.int32, idx.shape, 0)
            qcol = lax.broadcasted_iota(jnp.int32, idx.shape, 1)
            acc = jnp.where(krow // CHUNK <= qcol // CHUNK, acc, NEG_INF)
        o_ref[0, t] = jnp.concatenate([acc, acc], axis=1)


def _bias_tiles(rel_bias):
    assert ATT_TQ == ATT_TK and ATT_TQ % MAX_DISTANCE == 0 and ATT_TQ % CHUNK == 0
    assert MAX_DISTANCE % LANES == 0
    kpos = jnp.arange(ATT_TK)[:, None]
    qpos = jnp.arange(ATT_TQ)[None, :]
    idx = jnp.stack([_t5_bucket(kpos - qpos), _t5_bucket(kpos - ATT_TK - qpos)]).astype(jnp.int32)
    return pl.pallas_call(
        _bias_kernel,
        grid=(H_A,),
        in_specs=[
            pl.BlockSpec(memory_space=pltpu.SMEM),
            pl.BlockSpec((2, ATT_TK, ATT_TQ), lambda h: (0, 0, 0)),
        ],
        out_specs=pl.BlockSpec((1, 2, ATT_TK, 2 * ATT_TQ), lambda h: (h, 0, 0, 0)),
        out_shape=jax.ShapeDtypeStruct((H_A, 2, ATT_TK, 2 * ATT_TQ), F32),
        name="bias_tiles",
    )(rel_bias.astype(F32), idx)


def _attn_kernel(lam_ref, qt_ref, k_ref, vt_ref, b_ref, g_ref, o_ref,
                 qq_ref, m_ref, acc_ref, ahead_ref, *, out_scale):
    qi = pl.program_id(1)
    tq = qt_ref.shape[1]
    half = LANES // 2

    zero = jnp.zeros((half, tq), BF16)
    for h in range(H_A):
        qt = qt_ref[h * LANES:(h + 1) * LANES, :]
        qq_ref[h] = jnp.concatenate(
            [jnp.concatenate([qt[:half], zero], axis=0),
             jnp.concatenate([zero, qt[half:]], axis=0)], axis=1)
    m_ref[...] = jnp.full(m_ref.shape, NEG_INF, F32)
    acc_ref[...] = jnp.zeros(acc_ref.shape, F32)

    def scores(j, h):
        start = pl.multiple_of(j * ATT_TK, ATT_TK)
        kj = k_ref[0, pl.ds(start, ATT_TK), h * LANES:(h + 1) * LANES]
        return jnp.dot(kj, qq_ref[h], preferred_element_type=F32)

    def biased(x, h, bias_tile):
        if bias_tile is None:
            return x
        if bias_tile == 0:
            return x + b_ref[h, 0]
        z = ATT_TK - MAX_DISTANCE
        pieces = []
        for c0 in range(0, 2 * tq, MAX_DISTANCE):
            blk = x[z:, c0:c0 + MAX_DISTANCE]
            if c0 % tq == 0:
                blk = blk + b_ref[h, 1, z:, c0:c0 + MAX_DISTANCE]
            pieces.append(blk)
        return jnp.concatenate([x[:z], jnp.concatenate(pieces, axis=1)], axis=0)

    def update(j, h, s, bias_tile):
        m_prev = m_ref[h]
        m_next = jnp.maximum(m_prev, jnp.max(biased(s, h, bias_tile), axis=0, keepdims=True))
        p = jnp.exp2(biased(s - m_next, h, bias_tile)).astype(BF16)
        alpha = jnp.exp2(m_prev - m_next)
        vj = vt_ref[j, h * V_ROWS:(h + 1) * V_ROWS, :]
        acc_ref[h] = alpha * acc_ref[h] + jnp.dot(vj, p, preferred_element_type=F32)
        m_ref[h] = m_next

    def run(blocks, next_block):
        items = [(j, h, bt) for j, bt in blocks for h in range(H_A)]
        pending = [ahead_ref[n] for n in range(QK_AHEAD)]
        for n, (j, h, bt) in enumerate(items):
            k = n + QK_AHEAD
            if k < len(items):
                pending.append(scores(items[k][0], items[k][1]))
            elif next_block is not None:
                ahead_ref[k - len(items)] = scores(next_block, k - len(items))
            update(j, h, pending.pop(0), bt)

    for n in range(QK_AHEAD):
        ahead_ref[n] = scores(0, n)
    n_far = jnp.maximum(qi - 1, 0)

    def far_pair(i, carry):
        run([(2 * i, None), (2 * i + 1, None)], 2 * i + 2)
        return carry

    lax.fori_loop(0, n_far // 2, far_pair, 0)

    @pl.when(n_far % 2 == 1)
    def _():
        run([(n_far - 1, None)], n_far)

    @pl.when(qi >= 1)
    def _():
        run([(qi - 1, 1), (qi, 0)], None)

    @pl.when(qi == 0)
    def _():
        run([(qi, 0)], None)

    for h in range(H_A):
        acc = acc_ref[h]
        ot = acc[:LANES] / acc[LANES:LANES + 1]
        o = jnp.transpose(ot[:, :tq] - lam_ref[0] * ot[:, tq:])
        o_ref[0, :, h * LANES:(h + 1) * LANES] = (
            _rms(o, g_ref[...]) * out_scale).astype(o_ref.dtype)


def _attention(zqt, zk, zvt, bias, lam, g, lam_init, b, s):
    att_w = zk.shape[1]
    nq = s // ATT_TQ
    nk = s // ATT_TK
    kernel = functools.partial(_attn_kernel, out_scale=1.0 - lam_init)
    return pl.pallas_call(
        kernel,
        grid=(b, nq),
        in_specs=[
            pl.BlockSpec(memory_space=pltpu.SMEM),
            pl.BlockSpec((att_w, ATT_TQ), lambda bi, i: (0, bi * nq + i)),
            pl.BlockSpec((1, s, att_w), lambda bi, i: (bi, 0, 0)),
            pl.BlockSpec((nk, H_A * V_ROWS, ATT_TK), lambda bi, i: (bi, 0, 0)),
            pl.BlockSpec((H_A, 2, ATT_TK, 2 * ATT_TQ), lambda bi, i: (0, 0, 0, 0),
                         pipeline_mode=pl.Buffered(1)),
            pl.BlockSpec((1, LANES), lambda bi, i: (0, 0)),
        ],
        out_specs=pl.BlockSpec((1, ATT_TQ, att_w), lambda bi, i: (bi, i, 0)),
        out_shape=jax.ShapeDtypeStruct((b, s, att_w), BF16),
        scratch_shapes=[
            pltpu.VMEM((H_A, LANES, 2 * ATT_TQ), BF16),
            pltpu.VMEM((H_A, 1, 2 * ATT_TQ), F32),
            pltpu.VMEM((H_A, V_ROWS, 2 * ATT_TQ), F32),
            pltpu.VMEM((QK_AHEAD, ATT_TK, 2 * ATT_TQ), F32),
        ],
        compiler_params=pltpu.CompilerParams(
            dimension_semantics=("parallel", "arbitrary"), vmem_limit_bytes=VMEM_LIMIT),
        name="diff_attention",
    )(lam, zqt, zk.reshape(b, s, att_w), zvt, bias, g)


def _cumsum_rows(g, tri3):
    hi = g.astype(BF16)
    r1 = g - hi.astype(F32)
    mid = r1.astype(BF16)
    lo = (r1 - mid.astype(F32)).astype(BF16)
    return jnp.dot(tri3, jnp.concatenate([hi, mid, lo], axis=0), preferred_element_type=F32)


def _rec_group(row_slices, rq_ref, rf_ref, ri_ref, rg_ref, llb_ref, l1m_ref, g_ref, o_ref,
               st_ref):
    c = CHUNK
    nt = (((1,), (1,)), ((), ()))
    tn = (((0,), (0,)), ((), ()))
    heads = [slice(h * LANES, (h + 1) * LANES) for h in range(H_R)]
    row = lax.broadcasted_iota(jnp.int32, (c, c), 0)
    col = lax.broadcasted_iota(jnp.int32, (c, c), 1)
    causal = col <= row
    tri = causal.astype(BF16)
    tri3 = jnp.concatenate([tri, tri, tri], axis=1)
    log_lb = llb_ref[...]
    log_1mlb = l1m_ref[...]

    chunks = []
    for rows in row_slices:
        rq = rq_ref[0, rows, :]
        rf2 = rf_ref[0, rows, :] * LOG2E
        ls = jnp.minimum(rf2, 0.0) - jnp.log2(1.0 + jnp.exp2(-jnp.abs(rf2)))
        bterm = log_1mlb + ls
        g = jnp.maximum(log_lb, bterm) + jnp.log2(1.0 + jnp.exp2(-jnp.abs(log_lb - bterm)))
        chunks.append(dict(g=g, kin=jnp.exp2(bterm - rf2),
                           q=rq / (1.0 + jnp.exp2(rq * -LOG2E)),
                           v=ri_ref[0, rows, :].astype(BF16)))
    for d in chunks:
        d["bc"] = _cumsum_rows(d["g"], tri3)

    for d in chunks:
        bc, q, kin = d["bc"], d["q"], d["kin"]
        blast = bc[c - 1:c]
        d["qe"] = (q * jnp.exp2(bc)).astype(BF16)
        d["ke"] = (kin * jnp.exp2(blast - bc)).astype(BF16)
        d["dl"] = jnp.exp2(blast)
        d["lhs"], d["rhs"] = [], []
        for i in range(c // SUB):
            lo, hi = i * SUB, (i + 1) * SUB
            mid = lo + SUB // 2
            u = jnp.clip(bc[lo:hi] - bc[mid:mid + 1], -EXP2_CLAMP, EXP2_CLAMP)
            qd = (q[lo:hi] * jnp.exp2(u)).astype(BF16)
            kd = (kin[lo:hi] * jnp.exp2(-u)).astype(BF16)
            kd_pad = jnp.concatenate(
                ([jnp.zeros((lo, kd.shape[1]), BF16)] if lo else []) + [kd]
                + ([jnp.zeros((c - hi, kd.shape[1]), BF16)] if c - hi else []), axis=0)
            if i == 0:
                d["lhs"].append((qd,))
                d["rhs"].append((kd_pad,))
            else:
                ref = bc[lo:lo + 1]
                qo = (q[lo:hi] * jnp.exp2(bc[lo:hi] - ref)).astype(BF16)
                ko = (kin[:lo] * jnp.exp2(ref - bc[:lo])).astype(BF16)
                ko_pad = jnp.concatenate([ko, jnp.zeros((c - lo, ko.shape[1]), BF16)], axis=0)
                d["lhs"].append((qo, qd))
                d["rhs"].append((ko_pad, kd_pad))

    for d in chunks:
        d["inc"] = [lax.dot_general(d["v"][:, hs], d["ke"][:, hs], tn, preferred_element_type=F32)
                    for hs in heads]
    for d in chunks:
        d["a"] = []
        for hs in heads:
            a_rows = [lax.dot_general(jnp.concatenate([p[:, hs] for p in lhs], axis=1),
                                      jnp.concatenate([p[:, hs] for p in rhs], axis=1),
                                      nt, preferred_element_type=F32)
                      for lhs, rhs in zip(d["lhs"], d["rhs"])]
            d["a"].append(jnp.where(causal, jnp.concatenate(a_rows, axis=0), 0.0).astype(BF16))
    states = [st_ref[h] for h in range(H_R)]
    for d in chunks:
        d["inter"] = [lax.dot_general(d["qe"][:, hs], states[h].astype(BF16), nt,
                                      preferred_element_type=F32)
                      for h, hs in enumerate(heads)]
        states = [states[h] * d["dl"][:, hs] + d["inc"][h] for h, hs in enumerate(heads)]
    for h in range(H_R):
        st_ref[h] = states[h]
    for rows, d in zip(row_slices, chunks):
        rg = rg_ref[0, rows, :]
        gate = rg / (1.0 + jnp.exp2(rg * -LOG2E))
        for h, hs in enumerate(heads):
            o = d["inter"][h] + jnp.dot(d["a"][h], d["v"][:, hs], preferred_element_type=F32)
            o_ref[0, rows, hs] = (_rms(o, g_ref[...]) * gate[:, hs]).astype(o_ref.dtype)


def _rec_kernel(rq_ref, rf_ref, ri_ref, rg_ref, llb_ref, l1m_ref, g_ref, o_ref, st_ref):
    @pl.when(pl.program_id(1) == 0)
    def _():
        st_ref[...] = jnp.zeros(st_ref.shape, F32)

    group = REC_GROUP * CHUNK

    def body(gi, carry):
        r0 = pl.multiple_of(gi * group, group)
        _rec_group([pl.ds(r0 + n * CHUNK, CHUNK) for n in range(REC_GROUP)],
                   rq_ref, rf_ref, ri_ref, rg_ref, llb_ref, l1m_ref, g_ref, o_ref, st_ref)
        return carry

    lax.fori_loop(0, rq_ref.shape[1] // group, body, 0)


def _recurrence(zr, log_lb, log_1mlb, g):
    b, s, cols4 = zr.shape
    w = cols4 // 4
    spec = lambda j: pl.BlockSpec((1, REC_TS, w), lambda bi, si, j=j: (bi, si, j))
    vec = lambda n: pl.BlockSpec((1, n), lambda bi, si: (0, 0))
    return pl.pallas_call(
        _rec_kernel,
        grid=(b, s // REC_TS),
        in_specs=[spec(0), spec(1), spec(2), spec(3), vec(w), vec(w), vec(LANES)],
        out_specs=pl.BlockSpec((1, REC_TS, w), lambda bi, si: (bi, si, 0)),
        out_shape=jax.ShapeDtypeStruct((b, s, w), BF16),
        scratch_shapes=[pltpu.VMEM((H_R, LANES, LANES), F32)],
        compiler_params=pltpu.CompilerParams(
            dimension_semantics=("parallel", "arbitrary"), vmem_limit_bytes=VMEM_LIMIT),
        name="hgrn2",
    )(zr, zr, zr, zr, log_lb, log_1mlb, g)


def _mlp_kernel(x_ref, oa_ref, or_ref, wo_ref, g2_ref, wu_ref, wd_ref, gf_ref, o_ref, u2_ref,
                *, final_norm):
    wa = oa_ref.shape[1]
    x = x_ref[...]
    x = x + jnp.dot(oa_ref[...], wo_ref[:wa, :], preferred_element_type=F32)
    x = x + jnp.dot(or_ref[...], wo_ref[wa:, :], preferred_element_type=F32)
    h2 = _rms(x, g2_ref[...]).astype(BF16)
    d_ff = wu_ref.shape[1]
    for f in range(d_ff // COL_TILE):
        cols = slice(f * COL_TILE, (f + 1) * COL_TILE)
        u = jnp.maximum(jnp.dot(h2, wu_ref[:, cols], preferred_element_type=F32), 0.0)
        u2_ref[:, cols] = (u * u).astype(BF16)
    x = x + jnp.dot(u2_ref[...], wd_ref[...], preferred_element_type=F32)
    if final_norm:
        x = _rms(x, gf_ref[...])
    o_ref[...] = x


def _mlp(x2d, oa, orr, wo, g2, wu, wd, gf, final_norm):
    n, d = x2d.shape
    wa, wr = oa.shape[1], orr.shape[1]
    d_ff = wu.shape[1]
    const = lambda shape: pl.BlockSpec(shape, lambda i: (0, 0), pipeline_mode=pl.Buffered(1))
    return pl.pallas_call(
        functools.partial(_mlp_kernel, final_norm=final_norm),
        grid=(n // ROW_TILE,),
        in_specs=[
            pl.BlockSpec((ROW_TILE, d), lambda i: (i, 0)),
            pl.BlockSpec((ROW_TILE, wa), lambda i: (i, 0)),
            pl.BlockSpec((ROW_TILE, wr), lambda i: (i, 0)),
            const((wa + wr, d)),
            const((1, d)),
            const((d, d_ff)),
            const((d_ff, d)),
            const((1, d)),
        ],
        out_specs=pl.BlockSpec((ROW_TILE, d), lambda i: (i, 0)),
        out_shape=jax.ShapeDtypeStruct((n, d), F32),
        scratch_shapes=[pltpu.VMEM((ROW_TILE, d_ff), BF16)],
        compiler_params=pltpu.CompilerParams(
            dimension_semantics=("parallel",), vmem_limit_bytes=VMEM_LIMIT),
        name="outproj_mlp",
    )(x2d, oa, orr, wo, g2, wu, wd, gf)


def kernel(x, norm1_g, w_in, lam_qk, attn_norm_g, lb_logits, hgrn_norm_g, w_out, norm2_g,
           w_up, w_down, rel_bias, final_g):
    b, s, d = x.shape
    depth = w_in.shape[0]
    att_w = attn_norm_g.shape[1] * H_A
    rec_w = hgrn_norm_g.shape[1] * H_R
    assert attn_norm_g.shape[1] == LANES and hgrn_norm_g.shape[1] == LANES
    assert w_in.shape[2] == 3 * att_w + 4 * rec_w
    assert (b * s) % ROW_TILE == 0 and s % ATT_TQ == 0 and s % REC_TS == 0

    lb = jnp.cumsum(jax.nn.softmax(lb_logits.astype(F32), axis=0), axis=0)
    lb = lb - lb[0:1]
    log_lb = jnp.log(lb) * LOG2E
    log_1mlb = jnp.log1p(-lb) * LOG2E

    bias = _bias_tiles(rel_bias)
    x2d = x.reshape(b * s, d)
    q_scale = (att_w // (2 * H_A)) ** -0.5 * LOG2E
    for l in range(depth):
        w = w_in[l].astype(BF16)
        zqt, zk, zvt, zr = _inproj(
            x2d, norm1_g[l][None], w[:, :att_w].T, w[:, att_w:2 * att_w],
            w[:, 2 * att_w:3 * att_w].T, w[:, 3 * att_w:], q_scale)

        lam_init = 0.8 - 0.6 * math.exp(-0.3 * l)
        lq = lam_qk[l].astype(F32)
        lam = jnp.exp(jnp.sum(lq[0] * lq[1])) - jnp.exp(jnp.sum(lq[2] * lq[3])) + lam_init
        oa = _attention(zqt, zk, zvt, bias, lam.reshape(1), attn_norm_g[l][None].astype(F32),
                        lam_init, b, s)
        orr = _recurrence(zr.reshape(b, s, 4 * rec_w), log_lb[l][None], log_1mlb[l][None],
                          hgrn_norm_g[l][None].astype(F32))
        x2d = _mlp(x2d, oa.reshape(b * s, att_w), orr.reshape(b * s, rec_w),
                   w_out[l].astype(BF16), norm2_g[l][None], w_up[l].astype(BF16),
                   w_down[l].astype(BF16), final_g[None], l == depth - 1)
    return x2d.reshape(b, s, d)
```

```python
import functools
import math

import jax
import jax.numpy as jnp
from jax import lax
from jax.experimental import pallas as pl
from jax.experimental.pallas import tpu as pltpu

CHUNK = 64
H_A = 4
H_R = 4
NUM_BUCKETS = 32
MAX_DISTANCE = 128
EPS = 1e-6
NEG_INF = -1e30

LANES = 128
ROW_TILE = 1024
COL_TILE = 512
ATT_TQ = 256
ATT_TK = 256
QK_AHEAD = 2
REC_TS = 1024
REC_GROUP = 8
SUB = 16
LOG2E = 1.4426950408889634
EXP2_CLAMP = 80.0 * LOG2E
ONES_ROWS = 16
V_ROWS = LANES + ONES_ROWS
VMEM_LIMIT = 56 * 1024 * 1024

F32 = jnp.float32
BF16 = jnp.bfloat16


def _rms(x, g):
    return x * lax.rsqrt(jnp.mean(x * x, axis=-1, keepdims=True) + EPS) * g


def _inproj_kernel(x_ref, g_ref, wqt_ref, wk_ref, wvt_ref, wr_ref,
                   zqt_ref, zk_ref, zvt_ref, zr_ref, *, q_scale):
    h = _rms(x_ref[...], g_ref[...]).astype(BF16)
    nt = (((1,), (1,)), ((), ()))
    zqt_ref[...] = (lax.dot_general(wqt_ref[...], h, nt, preferred_element_type=F32)
                    * q_scale).astype(BF16)
    zk_ref[...] = jnp.dot(h, wk_ref[...], preferred_element_type=F32).astype(BF16)
    zvt = lax.dot_general(wvt_ref[...], h, nt, preferred_element_type=F32).astype(BF16)
    ones = jnp.ones((ONES_ROWS, ATT_TK), BF16)
    for c in range(zvt_ref.shape[0]):
        for hd in range(H_A):
            zvt_ref[c, hd * V_ROWS:hd * V_ROWS + LANES, :] = (
                zvt[hd * LANES:(hd + 1) * LANES, c * ATT_TK:(c + 1) * ATT_TK])
            zvt_ref[c, hd * V_ROWS + LANES:(hd + 1) * V_ROWS, :] = ones
    for j in range(wr_ref.shape[1] // COL_TILE):
        cols = slice(j * COL_TILE, (j + 1) * COL_TILE)
        zr_ref[:, cols] = jnp.dot(h, wr_ref[:, cols], preferred_element_type=F32)


def _inproj(x2d, g, wqt, wk, wvt, wr, q_scale):
    n, d = x2d.shape
    att_w = wk.shape[1]
    rec_cols = wr.shape[1]
    const = lambda shape: pl.BlockSpec(shape, lambda i: (0,) * len(shape),
                                       pipeline_mode=pl.Buffered(1))
    return pl.pallas_call(
        functools.partial(_inproj_kernel, q_scale=q_scale),
        grid=(n // ROW_TILE,),
        in_specs=[
            pl.BlockSpec((ROW_TILE, d), lambda i: (i, 0)),
            const((1, d)), const((att_w, d)), const((d, att_w)), const((att_w, d)),
            const((d, rec_cols)),
        ],
        out_specs=[
            pl.BlockSpec((att_w, ROW_TILE), lambda i: (0, i)),
            pl.BlockSpec((ROW_TILE, att_w), lambda i: (i, 0)),
            pl.BlockSpec((ROW_TILE // ATT_TK, H_A * V_ROWS, ATT_TK), lambda i: (i, 0, 0)),
            pl.BlockSpec((ROW_TILE, rec_cols), lambda i: (i, 0)),
        ],
        out_shape=[
            jax.ShapeDtypeStruct((att_w, n), BF16),
            jax.ShapeDtypeStruct((n, att_w), BF16),
            jax.ShapeDtypeStruct((n // ATT_TK, H_A * V_ROWS, ATT_TK), BF16),
            jax.ShapeDtypeStruct((n, rec_cols), F32),
        ],
        compiler_params=pltpu.CompilerParams(
            dimension_semantics=("parallel",), vmem_limit_bytes=VMEM_LIMIT),
        name="inproj",
    )(x2d, g, wqt, wk, wvt, wr)


def _t5_bucket(rel):
    n_half = NUM_BUCKETS // 2
    max_exact = n_half // 2
    ret = jnp.where(rel > 0, n_half, 0)
    n = jnp.abs(rel)
    nf = jnp.maximum(n, 1).astype(jnp.float32)
    large = max_exact + (jnp.log(nf / max_exact) / math.log(MAX_DISTANCE / max_exact)
                         * (n_half - max_exact)).astype(jnp.int32)
    large = jnp.minimum(large, n_half - 1)
    return ret + jnp.where(n < max_exact, n, large)


def _bias_kernel(rb_ref, idx_ref, o_ref):
    h = pl.program_id(0)
    far = rb_ref[NUM_BUCKETS // 2 - 1, h]
    for t in range(2):
        idx = idx_ref[t]
        acc = jnp.zeros(idx.shape, F32)
        for b in range(NUM_BUCKETS):
            acc = jnp.where(idx == b, (rb_ref[b, h] - far) * LOG2E, acc)
        if t == 0:
            krow = lax.broadcasted_iota(jnp.int32, idx.shape, 0)
            qcol = lax.broadcasted_iota(jnp.int32, idx.shape, 1)
            acc = jnp.where(krow // CHUNK <= qcol // CHUNK, acc, NEG_INF)
        o_ref[0, t] = jnp.concatenate([acc, acc], axis=1)


def _bias_tiles(rel_bias):
    assert ATT_TQ == ATT_TK and ATT_TQ % MAX_DISTANCE == 0 and ATT_TQ % CHUNK == 0
    assert MAX_DISTANCE % LANES == 0
    kpos = jnp.arange(ATT_TK)[:, None]
    qpos = jnp.arange(ATT_TQ)[None, :]
    idx = jnp.stack([_t5_bucket(kpos - qpos), _t5_bucket(kpos - ATT_TK - qpos)]).astype(jnp.int32)
    return pl.pallas_call(
        _bias_kernel,
        grid=(H_A,),
        in_specs=[
            pl.BlockSpec(memory_space=pltpu.SMEM),
            pl.BlockSpec((2, ATT_TK, ATT_TQ), lambda h: (0, 0, 0)),
        ],
        out_specs=pl.BlockSpec((1, 2, ATT_TK, 2 * ATT_TQ), lambda h: (h, 0, 0, 0)),
        out_shape=jax.ShapeDtypeStruct((H_A, 2, ATT_TK, 2 * ATT_TQ), F32),
        name="bias_tiles",
    )(rel_bias.astype(F32), idx)


def _attn_kernel(lam_ref, qt_ref, k_ref, vt_ref, b_ref, g_ref, o_ref,
                 qq_ref, m_ref, acc_ref, ahead_ref, *, out_scale):
    qi = pl.program_id(1)
    tq = qt_ref.shape[1]
    half = LANES // 2

    zero = jnp.zeros((half, tq), BF16)
    for h in range(H_A):
        qt = qt_ref[h * LANES:(h + 1) * LANES, :]
        qq_ref[h] = jnp.concatenate(
            [jnp.concatenate([qt[:half], zero], axis=0),
             jnp.concatenate([zero, qt[half:]], axis=0)], axis=1)
    m_ref[...] = jnp.full(m_ref.shape, NEG_INF, F32)
    acc_ref[...] = jnp.zeros(acc_ref.shape, F32)

    def scores(j, h):
        start = pl.multiple_of(j * ATT_TK, ATT_TK)
        kj = k_ref[0, pl.ds(start, ATT_TK), h * LANES:(h + 1) * LANES]
        return jnp.dot(kj, qq_ref[h], preferred_element_type=F32)

    def biased(x, h, bias_tile):
        if bias_tile is None:
            return x
        if bias_tile == 0:
            return x + b_ref[h, 0]
        z = ATT_TK - MAX_DISTANCE
        pieces = []
        for c0 in range(0, 2 * tq, MAX_DISTANCE):
            blk = x[z:, c0:c0 + MAX_DISTANCE]
            if c0 % tq == 0:
                blk = blk + b_ref[h, 1, z:, c0:c0 + MAX_DISTANCE]
            pieces.append(blk)
        return jnp.concatenate([x[:z], jnp.concatenate(pieces, axis=1)], axis=0)

    def update(j, h, s, bias_tile):
        m_prev = m_ref[h]
        m_next = jnp.maximum(m_prev, jnp.max(biased(s, h, bias_tile), axis=0, keepdims=True))
        p = jnp.exp2(biased(s - m_next, h, bias_tile)).astype(BF16)
        alpha = jnp.exp2(m_prev - m_next)
        vj = vt_ref[j, h * V_ROWS:(h + 1) * V_ROWS, :]
        acc_ref[h] = alpha * acc_ref[h] + jnp.dot(vj, p, preferred_element_type=F32)
        m_ref[h] = m_next

    def run(blocks, next_block):
        items = [(j, h, bt) for j, bt in blocks for h in range(H_A)]
        pending = [ahead_ref[n] for n in range(QK_AHEAD)]
        for n, (j, h, bt) in enumerate(items):
            k = n + QK_AHEAD
            if k < len(items):
                pending.append(scores(items[k][0], items[k][1]))
            elif next_block is not None:
                ahead_ref[k - len(items)] = scores(next_block, k - len(items))
            update(j, h, pending.pop(0), bt)

    for n in range(QK_AHEAD):
        ahead_ref[n] = scores(0, n)
    n_far = jnp.maximum(qi - 1, 0)

    def far_quad(i, carry):
        run([(4 * i + n, None) for n in range(4)], 4 * i + 4)
        return carry

    lax.fori_loop(0, n_far // 4, far_quad, 0)
    quad_end = (n_far // 4) * 4

    @pl.when(n_far % 4 >= 2)
    def _():
        run([(quad_end, None), (quad_end + 1, None)], quad_end + 2)

    @pl.when(n_far % 2 == 1)
    def _():
        run([(n_far - 1, None)], n_far)

    @pl.when(qi >= 1)
    def _():
        run([(qi - 1, 1), (qi, 0)], None)

    @pl.when(qi == 0)
    def _():
        run([(qi, 0)], None)

    gs = g_ref[...] * out_scale
    for h in range(H_A):
        acc = acc_ref[h]
        inv = 1.0 / acc[LANES:LANES + 1]
        ot = acc[:LANES, :tq] * inv[:, :tq] - acc[:LANES, tq:] * (lam_ref[0] * inv[:, tq:])
        ot = ot * lax.rsqrt(jnp.mean(ot * ot, axis=0, keepdims=True) + EPS)
        o_ref[0, :, h * LANES:(h + 1) * LANES] = (jnp.transpose(ot) * gs).astype(o_ref.dtype)


def _attention(zqt, zk, zvt, bias, lam, g, lam_init, b, s):
    att_w = zk.shape[1]
    nq = s // ATT_TQ
    nk = s // ATT_TK
    kernel = functools.partial(_attn_kernel, out_scale=1.0 - lam_init)
    return pl.pallas_call(
        kernel,
        grid=(b, nq),
        in_specs=[
            pl.BlockSpec(memory_space=pltpu.SMEM),
            pl.BlockSpec((att_w, ATT_TQ), lambda bi, i: (0, bi * nq + i)),
            pl.BlockSpec((1, s, att_w), lambda bi, i: (bi, 0, 0)),
            pl.BlockSpec((nk, H_A * V_ROWS, ATT_TK), lambda bi, i: (bi, 0, 0)),
            pl.BlockSpec((H_A, 2, ATT_TK, 2 * ATT_TQ), lambda bi, i: (0, 0, 0, 0),
                         pipeline_mode=pl.Buffered(1)),
            pl.BlockSpec((1, LANES), lambda bi, i: (0, 0)),
        ],
        out_specs=pl.BlockSpec((1, ATT_TQ, att_w), lambda bi, i: (bi, i, 0)),
        out_shape=jax.ShapeDtypeStruct((b, s, att_w), BF16),
        scratch_shapes=[
            pltpu.VMEM((H_A, LANES, 2 * ATT_TQ), BF16),
            pltpu.VMEM((H_A, 1, 2 * ATT_TQ), F32),
            pltpu.VMEM((H_A, V_ROWS, 2 * ATT_TQ), F32),
            pltpu.VMEM((QK_AHEAD, ATT_TK, 2 * ATT_TQ), F32),
        ],
        compiler_params=pltpu.CompilerParams(
            dimension_semantics=("parallel", "arbitrary"), vmem_limit_bytes=VMEM_LIMIT),
        name="diff_attention",
    )(lam, zqt, zk.reshape(b, s, att_w), zvt, bias, g)


def _cumsum_rows(g, tri3):
    hi = g.astype(BF16)
    r1 = g - hi.astype(F32)
    mid = r1.astype(BF16)
    lo = (r1 - mid.astype(F32)).astype(BF16)
    return jnp.dot(tri3, jnp.concatenate([hi, mid, lo], axis=0), preferred_element_type=F32)


def _rec_group(row_slices, rq_ref, rf_ref, ri_ref, rg_ref, llb_ref, l1m_ref, g_ref, o_ref,
               st_ref):
    c = CHUNK
    nt = (((1,), (1,)), ((), ()))
    tn = (((0,), (0,)), ((), ()))
    heads = [slice(h * LANES, (h + 1) * LANES) for h in range(H_R)]
    row = lax.broadcasted_iota(jnp.int32, (c, c), 0)
    col = lax.broadcasted_iota(jnp.int32, (c, c), 1)
    causal = col <= row
    tri = causal.astype(BF16)
    tri3 = jnp.concatenate([tri, tri, tri], axis=1)
    log_lb = llb_ref[...]
    log_1mlb = l1m_ref[...]

    chunks = []
    for rows in row_slices:
        rq = rq_ref[0, rows, :]
        rf2 = rf_ref[0, rows, :] * LOG2E
        ls = jnp.minimum(rf2, 0.0) - jnp.log2(1.0 + jnp.exp2(-jnp.abs(rf2)))
        bterm = log_1mlb + ls
        g = jnp.maximum(log_lb, bterm) + jnp.log2(1.0 + jnp.exp2(-jnp.abs(log_lb - bterm)))
        chunks.append(dict(g=g, kin=jnp.exp2(bterm - rf2),
                           q=rq / (1.0 + jnp.exp2(rq * -LOG2E)),
                           v=ri_ref[0, rows, :].astype(BF16)))
    for d in chunks:
        d["bc"] = _cumsum_rows(d["g"], tri3)

    for d in chunks:
        bc, q, kin = d["bc"], d["q"], d["kin"]
        blast = bc[c - 1:c]
        d["qe"] = (q * jnp.exp2(bc)).astype(BF16)
        d["ke"] = (kin * jnp.exp2(blast - bc)).astype(BF16)
        d["dl"] = jnp.exp2(blast)
        d["lhs"], d["rhs"] = [], []
        for i in range(c // SUB):
            lo, hi = i * SUB, (i + 1) * SUB
            mid = lo + SUB // 2
            u = jnp.clip(bc[lo:hi] - bc[mid:mid + 1], -EXP2_CLAMP, EXP2_CLAMP)
            qd = (q[lo:hi] * jnp.exp2(u)).astype(BF16)
            kd = (kin[lo:hi] * jnp.exp2(-u)).astype(BF16)
            kd_pad = jnp.concatenate(
                ([jnp.zeros((lo, kd.shape[1]), BF16)] if lo else []) + [kd]
                + ([jnp.zeros((c - hi, kd.shape[1]), BF16)] if c - hi else []), axis=0)
            if i == 0:
                d["lhs"].append((qd,))
                d["rhs"].append((kd_pad,))
            else:
                ref = bc[lo:lo + 1]
                qo = (q[lo:hi] * jnp.exp2(bc[lo:hi] - ref)).astype(BF16)
                ko = (kin[:lo] * jnp.exp2(ref - bc[:lo])).astype(BF16)
                ko_pad = jnp.concatenate([ko, jnp.zeros((c - lo, ko.shape[1]), BF16)], axis=0)
                d["lhs"].append((qo, qd))
                d["rhs"].append((ko_pad, kd_pad))

    for d in chunks:
        d["inc"] = [lax.dot_general(d["v"][:, hs], d["ke"][:, hs], tn, preferred_element_type=F32)
                    for hs in heads]
    for d in chunks:
        d["a"] = []
        for hs in heads:
            a_rows = [lax.dot_general(jnp.concatenate([p[:, hs] for p in lhs], axis=1),
                                      jnp.concatenate([p[:, hs] for p in rhs], axis=1),
                                      nt, preferred_element_type=F32)
                      for lhs, rhs in zip(d["lhs"], d["rhs"])]
            d["a"].append(jnp.where(causal, jnp.concatenate(a_rows, axis=0), 0.0).astype(BF16))
    states = [st_ref[h] for h in range(H_R)]
    for d in chunks:
        d["inter"] = [lax.dot_general(d["qe"][:, hs], states[h].astype(BF16), nt,
                                      preferred_element_type=F32)
                      for h, hs in enumerate(heads)]
        states = [states[h] * d["dl"][:, hs] + d["inc"][h] for h, hs in enumerate(heads)]
    for h in range(H_R):
        st_ref[h] = states[h]
    for rows, d in zip(row_slices, chunks):
        rg = rg_ref[0, rows, :]
        gate = rg / (1.0 + jnp.exp2(rg * -LOG2E))
        for h, hs in enumerate(heads):
            o = d["inter"][h] + jnp.dot(d["a"][h], d["v"][:, hs], preferred_element_type=F32)
            o_ref[0, rows, hs] = (_rms(o, g_ref[...]) * gate[:, hs]).astype(o_ref.dtype)


def _rec_kernel(rq_ref, rf_ref, ri_ref, rg_ref, llb_ref, l1m_ref, g_ref, o_ref, st_ref):
    @pl.when(pl.program_id(1) == 0)
    def _():
        st_ref[...] = jnp.zeros(st_ref.shape, F32)

    group = REC_GROUP * CHUNK

    def body(gi, carry):
        r0 = pl.multiple_of(gi * group, group)
        _rec_group([pl.ds(r0 + n * CHUNK, CHUNK) for n in range(REC_GROUP)],
                   rq_ref, rf_ref, ri_ref, rg_ref, llb_ref, l1m_ref, g_ref, o_ref, st_ref)
        return carry

    lax.fori_loop(0, rq_ref.shape[1] // group, body, 0)


def _recurrence(zr, log_lb, log_1mlb, g):
    b, s, cols4 = zr.shape
    w = cols4 // 4
    spec = lambda j: pl.BlockSpec((1, REC_TS, w), lambda bi, si, j=j: (bi, si, j))
    vec = lambda n: pl.BlockSpec((1, n), lambda bi, si: (0, 0))
    return pl.pallas_call(
        _rec_kernel,
        grid=(b, s // REC_TS),
        in_specs=[spec(0), spec(1), spec(2), spec(3), vec(w), vec(w), vec(LANES)],
        out_specs=pl.BlockSpec((1, REC_TS, w), lambda bi, si: (bi, si, 0)),
        out_shape=jax.ShapeDtypeStruct((b, s, w), BF16),
        scratch_shapes=[pltpu.VMEM((H_R, LANES, LANES), F32)],
        compiler_params=pltpu.CompilerParams(
            dimension_semantics=("parallel", "arbitrary"), vmem_limit_bytes=VMEM_LIMIT),
        name="hgrn2",
    )(zr, zr, zr, zr, log_lb, log_1mlb, g)


def _mlp_kernel(x_ref, oa_ref, or_ref, wo_ref, g2_ref, wu_ref, wd_ref, gf_ref, o_ref, u2_ref,
                *, final_norm):
    wa = oa_ref.shape[1]
    x = x_ref[...]
    x = x + jnp.dot(oa_ref[...], wo_ref[:wa, :], preferred_element_type=F32)
    x = x + jnp.dot(or_ref[...], wo_ref[wa:, :], preferred_element_type=F32)
    h2 = _rms(x, g2_ref[...]).astype(BF16)
    d_ff = wu_ref.shape[1]
    for f in range(d_ff // COL_TILE):
        cols = slice(f * COL_TILE, (f + 1) * COL_TILE)
        u = jnp.maximum(jnp.dot(h2, wu_ref[:, cols], preferred_element_type=F32), 0.0)
        u2_ref[:, cols] = (u * u).astype(BF16)
    x = x + jnp.dot(u2_ref[...], wd_ref[...], preferred_element_type=F32)
    if final_norm:
        x = _rms(x, gf_ref[...])
    o_ref[...] = x


def _mlp(x2d, oa, orr, wo, g2, wu, wd, gf, final_norm):
    n, d = x2d.shape
    wa, wr = oa.shape[1], orr.shape[1]
    d_ff = wu.shape[1]
    const = lambda shape: pl.BlockSpec(shape, lambda i: (0, 0), pipeline_mode=pl.Buffered(1))
    return pl.pallas_call(
        functools.partial(_mlp_kernel, final_norm=final_norm),
        grid=(n // ROW_TILE,),
        in_specs=[
            pl.BlockSpec((ROW_TILE, d), lambda i: (i, 0)),
            pl.BlockSpec((ROW_TILE, wa), lambda i: (i, 0)),
            pl.BlockSpec((ROW_TILE, wr), lambda i: (i, 0)),
            const((wa + wr, d)),
            const((1, d)),
            const((d, d_ff)),
            const((d_ff, d)),
            const((1, d)),
        ],
        out_specs=pl.BlockSpec((ROW_TILE, d), lambda i: (i, 0)),
        out_shape=jax.ShapeDtypeStruct((n, d), F32),
        scratch_shapes=[pltpu.VMEM((ROW_TILE, d_ff), BF16)],
        compiler_params=pltpu.CompilerParams(
            dimension_semantics=("parallel",), vmem_limit_bytes=VMEM_LIMIT),
        name="outproj_mlp",
    )(x2d, oa, orr, wo, g2, wu, wd, gf)


def kernel(x, norm1_g, w_in, lam_qk, attn_norm_g, lb_logits, hgrn_norm_g, w_out, norm2_g,
           w_up, w_down, rel_bias, final_g):
    b, s, d = x.shape
    depth = w_in.shape[0]
    att_w = attn_norm_g.shape[1] * H_A
    rec_w = hgrn_norm_g.shape[1] * H_R
    assert attn_norm_g.shape[1] == LANES and hgrn_norm_g.shape[1] == LANES
    assert w_in.shape[2] == 3 * att_w + 4 * rec_w
    assert (b * s) % ROW_TILE == 0 and s % ATT_TQ == 0 and s % REC_TS == 0

    lb = jnp.cumsum(jax.nn.softmax(lb_logits.astype(F32), axis=0), axis=0)
    lb = lb - lb[0:1]
    log_lb = jnp.log(lb) * LOG2E
    log_1mlb = jnp.log1p(-lb) * LOG2E

    bias = _bias_tiles(rel_bias)
    x2d = x.reshape(b * s, d)
    q_scale = (att_w // (2 * H_A)) ** -0.5 * LOG2E
    for l in range(depth):
        w = w_in[l].astype(BF16)
        zqt, zk, zvt, zr = _inproj(
            x2d, norm1_g[l][None], w[:, :att_w].T, w[:, att_w:2 * att_w],
            w[:, 2 * att_w:3 * att_w].T, w[:, 3 * att_w:], q_scale)

        lam_init = 0.8 - 0.6 * math.exp(-0.3 * l)
        lq = lam_qk[l].astype(F32)
        lam = jnp.exp(jnp.sum(lq[0] * lq[1])) - jnp.exp(jnp.sum(lq[2] * lq[3])) + lam_init
        oa = _attention(zqt, zk, zvt, bias, lam.reshape(1), attn_norm_g[l][None].astype(F32),
                        lam_init, b, s)
        orr = _recurrence(zr.reshape(b, s, 4 * rec_w), log_lb[l][None], log_1mlb[l][None],
                          hgrn_norm_g[l][None].astype(F32))
        x2d = _mlp(x2d, oa.reshape(b * s, att_w), orr.reshape(b * s, rec_w),
                   w_out[l].astype(BF16), norm2_g[l][None], w_up[l].astype(BF16),
                   w_down[l].astype(BF16), final_g[None], l == depth - 1)
    return x2d.reshape(b, s, d)
```

```python
import functools
import math

import jax
import jax.numpy as jnp
from jax import lax
from jax.experimental import pallas as pl
from jax.experimental.pallas import tpu as pltpu

CHUNK = 64
H_A = 4
H_R = 4
NUM_BUCKETS = 32
MAX_DISTANCE = 128
EPS = 1e-6
NEG_INF = -1e30

LANES = 128
ROW_TILE = 1024
COL_TILE = 512
ATT_TQ = 256
ATT_TK = 256
QK_AHEAD = 2
REC_TS = 1024
REC_GROUP = 8
SUB = 16
LOG2E = 1.4426950408889634
EXP2_CLAMP = 80.0 * LOG2E
ONES_ROWS = 16
V_ROWS = LANES + ONES_ROWS
VMEM_LIMIT = 56 * 1024 * 1024

F32 = jnp.float32
BF16 = jnp.bfloat16


def _rms(x, g):
    return x * lax.rsqrt(jnp.mean(x * x, axis=-1, keepdims=True) + EPS) * g


def _inproj_kernel(x_ref, g_ref, wqt_ref, wk_ref, wvt_ref, wr_ref,
                   zqt_ref, zk_ref, zvt_ref, zr_ref, *, q_scale):
    h = _rms(x_ref[...], g_ref[...]).astype(BF16)
    nt = (((1,), (1,)), ((), ()))
    zqt_ref[...] = (lax.dot_general(wqt_ref[...], h, nt, preferred_element_type=F32)
                    * q_scale).astype(BF16)
    zk_ref[...] = jnp.dot(h, wk_ref[...], preferred_element_type=F32).astype(BF16)
    zvt = lax.dot_general(wvt_ref[...], h, nt, preferred_element_type=F32).astype(BF16)
    ones = jnp.ones((ONES_ROWS, ATT_TK), BF16)
    for c in range(zvt_ref.shape[0]):
        for hd in range(H_A):
            zvt_ref[c, hd * V_ROWS:hd * V_ROWS + LANES, :] = (
                zvt[hd * LANES:(hd + 1) * LANES, c * ATT_TK:(c + 1) * ATT_TK])
            zvt_ref[c, hd * V_ROWS + LANES:(hd + 1) * V_ROWS, :] = ones
    for j in range(wr_ref.shape[1] // COL_TILE):
        cols = slice(j * COL_TILE, (j + 1) * COL_TILE)
        zr_ref[:, cols] = jnp.dot(h, wr_ref[:, cols], preferred_element_type=F32)


def _inproj(x2d, g, wqt, wk, wvt, wr, q_scale):
    n, d = x2d.shape
    att_w = wk.shape[1]
    rec_cols = wr.shape[1]
    const = lambda shape: pl.BlockSpec(shape, lambda i: (0,) * len(shape),
                                       pipeline_mode=pl.Buffered(1))
    return pl.pallas_call(
        functools.partial(_inproj_kernel, q_scale=q_scale),
        grid=(n // ROW_TILE,),
        in_specs=[
            pl.BlockSpec((ROW_TILE, d), lambda i: (i, 0)),
            const((1, d)), const((att_w, d)), const((d, att_w)), const((att_w, d)),
            const((d, rec_cols)),
        ],
        out_specs=[
            pl.BlockSpec((att_w, ROW_TILE), lambda i: (0, i)),
            pl.BlockSpec((ROW_TILE, att_w), lambda i: (i, 0)),
            pl.BlockSpec((ROW_TILE // ATT_TK, H_A * V_ROWS, ATT_TK), lambda i: (i, 0, 0)),
            pl.BlockSpec((ROW_TILE, rec_cols), lambda i: (i, 0)),
        ],
        out_shape=[
            jax.ShapeDtypeStruct((att_w, n), BF16),
            jax.ShapeDtypeStruct((n, att_w), BF16),
            jax.ShapeDtypeStruct((n // ATT_TK, H_A * V_ROWS, ATT_TK), BF16),
            jax.ShapeDtypeStruct((n, rec_cols), F32),
        ],
        compiler_params=pltpu.CompilerParams(
            dimension_semantics=("parallel",), vmem_limit_bytes=VMEM_LIMIT),
        name="inproj",
    )(x2d, g, wqt, wk, wvt, wr)


def _t5_bucket(rel):
    n_half = NUM_BUCKETS // 2
    max_exact = n_half // 2
    ret = jnp.where(rel > 0, n_half, 0)
    n = jnp.abs(rel)
    nf = jnp.maximum(n, 1).astype(jnp.float32)
    large = max_exact + (jnp.log(nf / max_exact) / math.log(MAX_DISTANCE / max_exact)
                         * (n_half - max_exact)).astype(jnp.int32)
    large = jnp.minimum(large, n_half - 1)
    return ret + jnp.where(n < max_exact, n, large)


def _bias_kernel(rb_ref, idx_ref, o_ref):
    h = pl.program_id(0)
    far = rb_ref[NUM_BUCKETS // 2 - 1, h]
    for t in range(2):
        idx = idx_ref[t]
        acc = jnp.zeros(idx.shape, F32)
        for b in range(NUM_BUCKETS):
            acc = jnp.where(idx == b, (rb_ref[b, h] - far) * LOG2E, acc)
        if t == 0:
            krow = lax.broadcasted_iota(jnp.int32, idx.shape, 0)
            qcol = lax.broadcasted_iota(jnp.int32, idx.shape, 1)
            acc = jnp.where(krow // CHUNK <= qcol // CHUNK, acc, NEG_INF)
        o_ref[0, t] = jnp.concatenate([acc, acc], axis=1)


def _bias_tiles(rel_bias):
    assert ATT_TQ == ATT_TK and ATT_TQ % MAX_DISTANCE == 0 and ATT_TQ % CHUNK == 0
    assert MAX_DISTANCE % LANES == 0
    kpos = jnp.arange(ATT_TK)[:, None]
    qpos = jnp.arange(ATT_TQ)[None, :]
    idx = jnp.stack([_t5_bucket(kpos - qpos), _t5_bucket(kpos - ATT_TK - qpos)]).astype(jnp.int32)
    return pl.pallas_call(
        _bias_kernel,
        grid=(H_A,),
        in_specs=[
            pl.BlockSpec(memory_space=pltpu.SMEM),
            pl.BlockSpec((2, ATT_TK, ATT_TQ), lambda h: (0, 0, 0)),
        ],
        out_specs=pl.BlockSpec((1, 2, ATT_TK, 2 * ATT_TQ), lambda h: (h, 0, 0, 0)),
        out_shape=jax.ShapeDtypeStruct((H_A, 2, ATT_TK, 2 * ATT_TQ), F32),
        name="bias_tiles",
    )(rel_bias.astype(F32), idx)


def _attn_kernel(lam_ref, qt_ref, qn_ref, k_ref, vt_ref, b_ref, g_ref, o_ref,
                 qq_ref, m_ref, acc_ref, ahead_ref, *, out_scale):
    qi = pl.program_id(1)
    tq = qt_ref.shape[1]
    half = LANES // 2
    zero = jnp.zeros((half, tq), BF16)

    def both_maps(q_ref, h):
        qt = q_ref[h * LANES:(h + 1) * LANES, :]
        return jnp.concatenate([jnp.concatenate([qt[:half], zero], axis=0),
                                jnp.concatenate([zero, qt[half:]], axis=0)], axis=1)

    for h in range(H_A):
        qq_ref[h] = both_maps(qt_ref, h)
    m_ref[...] = jnp.full(m_ref.shape, NEG_INF, F32)
    acc_ref[...] = jnp.zeros(acc_ref.shape, F32)

    def scores(j, h, qq=None):
        start = pl.multiple_of(j * ATT_TK, ATT_TK)
        kj = k_ref[0, pl.ds(start, ATT_TK), h * LANES:(h + 1) * LANES]
        return jnp.dot(kj, qq_ref[h] if qq is None else qq,
                       preferred_element_type=F32)

    def biased(x, h, bias_tile):
        if bias_tile is None:
            return x
        if bias_tile == 0:
            return x + b_ref[h, 0]
        z = ATT_TK - MAX_DISTANCE
        pieces = []
        for c0 in range(0, 2 * tq, MAX_DISTANCE):
            blk = x[z:, c0:c0 + MAX_DISTANCE]
            if c0 % tq == 0:
                blk = blk + b_ref[h, 1, z:, c0:c0 + MAX_DISTANCE]
            pieces.append(blk)
        return jnp.concatenate([x[:z], jnp.concatenate(pieces, axis=1)], axis=0)

    def update(j, h, s, bias_tile):
        m_prev = m_ref[h]
        m_next = jnp.maximum(m_prev, jnp.max(biased(s, h, bias_tile), axis=0, keepdims=True))
        p = jnp.exp2(biased(s - m_next, h, bias_tile)).astype(BF16)
        alpha = jnp.exp2(m_prev - m_next)
        vj = vt_ref[j, h * V_ROWS:(h + 1) * V_ROWS, :]
        acc_ref[h] = alpha * acc_ref[h] + jnp.dot(vj, p, preferred_element_type=F32)
        m_ref[h] = m_next

    def run(blocks, next_block):
        items = [(j, h, bt) for j, bt in blocks for h in range(H_A)]
        pending = [ahead_ref[n] for n in range(QK_AHEAD)]
        for n, (j, h, bt) in enumerate(items):
            k = n + QK_AHEAD
            if k < len(items):
                pending.append(scores(items[k][0], items[k][1]))
            elif next_block is not None:
                ahead_ref[k - len(items)] = scores(next_block, k - len(items))
            update(j, h, pending.pop(0), bt)

    @pl.when(qi == 0)
    def _():
        for n in range(QK_AHEAD):
            ahead_ref[n] = scores(0, n)

    n_far = jnp.maximum(qi - 1, 0)

    def far_quad(i, carry):
        run([(4 * i + n, None) for n in range(4)], 4 * i + 4)
        return carry

    lax.fori_loop(0, n_far // 4, far_quad, 0)
    quad_end = (n_far // 4) * 4

    @pl.when(n_far % 4 >= 2)
    def _():
        run([(quad_end, None), (quad_end + 1, None)], quad_end + 2)

    @pl.when(n_far % 2 == 1)
    def _():
        run([(n_far - 1, None)], n_far)

    @pl.when(qi >= 1)
    def _():
        run([(qi - 1, 1), (qi, 0)], None)

    @pl.when(qi == 0)
    def _():
        run([(qi, 0)], None)

    for n in range(QK_AHEAD):
        ahead_ref[n] = scores(0, n, both_maps(qn_ref, n))

    gs = g_ref[...] * out_scale
    for h in range(H_A):
        acc = acc_ref[h]
        inv = 1.0 / acc[LANES:LANES + 1]
        ot = acc[:LANES, :tq] * inv[:, :tq] - acc[:LANES, tq:] * (lam_ref[0] * inv[:, tq:])
        ot = ot * lax.rsqrt(jnp.mean(ot * ot, axis=0, keepdims=True) + EPS)
        o_ref[0, :, h * LANES:(h + 1) * LANES] = (jnp.transpose(ot) * gs).astype(o_ref.dtype)


def _attention(zqt, zk, zvt, bias, lam, g, lam_init, b, s):
    att_w = zk.shape[1]
    nq = s // ATT_TQ
    nk = s // ATT_TK
    kernel = functools.partial(_attn_kernel, out_scale=1.0 - lam_init)
    return pl.pallas_call(
        kernel,
        grid=(b, nq),
        in_specs=[
            pl.BlockSpec(memory_space=pltpu.SMEM),
            pl.BlockSpec((att_w, ATT_TQ), lambda bi, i: (0, bi * nq + i)),
            pl.BlockSpec((att_w, ATT_TQ), lambda bi, i: (0, bi * nq + jnp.minimum(i + 1, nq - 1))),
            pl.BlockSpec((1, s, att_w), lambda bi, i: (bi, 0, 0)),
            pl.BlockSpec((nk, H_A * V_ROWS, ATT_TK), lambda bi, i: (bi, 0, 0)),
            pl.BlockSpec((H_A, 2, ATT_TK, 2 * ATT_TQ), lambda bi, i: (0, 0, 0, 0),
                         pipeline_mode=pl.Buffered(1)),
            pl.BlockSpec((1, LANES), lambda bi, i: (0, 0)),
        ],
        out_specs=pl.BlockSpec((1, ATT_TQ, att_w), lambda bi, i: (bi, i, 0)),
        out_shape=jax.ShapeDtypeStruct((b, s, att_w), BF16),
        scratch_shapes=[
            pltpu.VMEM((H_A, LANES, 2 * ATT_TQ), BF16),
            pltpu.VMEM((H_A, 1, 2 * ATT_TQ), F32),
            pltpu.VMEM((H_A, V_ROWS, 2 * ATT_TQ), F32),
            pltpu.VMEM((QK_AHEAD, ATT_TK, 2 * ATT_TQ), F32),
        ],
        compiler_params=pltpu.CompilerParams(
            dimension_semantics=("parallel", "arbitrary"), vmem_limit_bytes=VMEM_LIMIT),
        name="diff_attention",
    )(lam, zqt, zqt, zk.reshape(b, s, att_w), zvt, bias, g)


def _cumsum_rows(g, tri3):
    hi = g.astype(BF16)
    r1 = g - hi.astype(F32)
    mid = r1.astype(BF16)
    lo = (r1 - mid.astype(F32)).astype(BF16)
    return jnp.dot(tri3, jnp.concatenate([hi, mid, lo], axis=0), preferred_element_type=F32)


def _rec_group(row_slices, rq_ref, rf_ref, ri_ref, rg_ref, llb_ref, l1m_ref, g_ref, o_ref,
               st_ref):
    c = CHUNK
    nt = (((1,), (1,)), ((), ()))
    tn = (((0,), (0,)), ((), ()))
    heads = [slice(h * LANES, (h + 1) * LANES) for h in range(H_R)]
    row = lax.broadcasted_iota(jnp.int32, (c, c), 0)
    col = lax.broadcasted_iota(jnp.int32, (c, c), 1)
    causal = col <= row
    tri = causal.astype(BF16)
    tri3 = jnp.concatenate([tri, tri, tri], axis=1)
    log_lb = llb_ref[...]
    log_1mlb = l1m_ref[...]

    chunks = []
    for rows in row_slices:
        rq = rq_ref[0, rows, :]
        rf2 = rf_ref[0, rows, :] * LOG2E
        ls = jnp.minimum(rf2, 0.0) - jnp.log2(1.0 + jnp.exp2(-jnp.abs(rf2)))
        bterm = log_1mlb + ls
        g = jnp.maximum(log_lb, bterm) + jnp.log2(1.0 + jnp.exp2(-jnp.abs(log_lb - bterm)))
        chunks.append(dict(g=g, kin=jnp.exp2(bterm - rf2),
                           q=rq / (1.0 + jnp.exp2(rq * -LOG2E)),
                           v=ri_ref[0, rows, :].astype(BF16)))
    for d in chunks:
        d["bc"] = _cumsum_rows(d["g"], tri3)

    for d in chunks:
        bc, q, kin = d["bc"], d["q"], d["kin"]
        blast = bc[c - 1:c]
        d["qe"] = (q * jnp.exp2(bc)).astype(BF16)
        d["ke"] = (kin * jnp.exp2(blast - bc)).astype(BF16)
        d["dl"] = jnp.exp2(blast)
        d["lhs"], d["rhs"] = [], []
        for i in range(c // SUB):
            lo, hi = i * SUB, (i + 1) * SUB
            mid = lo + SUB // 2
            u = jnp.clip(bc[lo:hi] - bc[mid:mid + 1], -EXP2_CLAMP, EXP2_CLAMP)
            qd = (q[lo:hi] * jnp.exp2(u)).astype(BF16)
            kd = (kin[lo:hi] * jnp.exp2(-u)).astype(BF16)
            kd_pad = jnp.concatenate(
                ([jnp.zeros((lo, kd.shape[1]), BF16)] if lo else []) + [kd]
                + ([jnp.zeros((c - hi, kd.shape[1]), BF16)] if c - hi else []), axis=0)
            if i == 0:
                d["lhs"].append((qd,))
                d["rhs"].append((kd_pad,))
            else:
                ref = bc[lo:lo + 1]
                qo = (q[lo:hi] * jnp.exp2(bc[lo:hi] - ref)).astype(BF16)
                ko = (kin[:lo] * jnp.exp2(ref - bc[:lo])).astype(BF16)
                ko_pad = jnp.concatenate([ko, jnp.zeros((c - lo, ko.shape[1]), BF16)], axis=0)
                d["lhs"].append((qo, qd))
                d["rhs"].append((ko_pad, kd_pad))

    for d in chunks:
        d["inc"] = [lax.dot_general(d["v"][:, hs], d["ke"][:, hs], tn, preferred_element_type=F32)
                    for hs in heads]
    for d in chunks:
        d["a"] = []
        for hs in heads:
            a_rows = [lax.dot_general(jnp.concatenate([p[:, hs] for p in lhs], axis=1),
                                      jnp.concatenate([p[:, hs] for p in rhs], axis=1),
                                      nt, preferred_element_type=F32)
                      for lhs, rhs in zip(d["lhs"], d["rhs"])]
            d["a"].append(jnp.where(causal, jnp.concatenate(a_rows, axis=0), 0.0).astype(BF16))
    states = [st_ref[h] for h in range(H_R)]
    for d in chunks:
        d["inter"] = [lax.dot_general(d["qe"][:, hs], states[h].astype(BF16), nt,
                                      preferred_element_type=F32)
                      for h, hs in enumerate(heads)]
        states = [states[h] * d["dl"][:, hs] + d["inc"][h] for h, hs in enumerate(heads)]
    for h in range(H_R):
        st_ref[h] = states[h]
    for rows, d in zip(row_slices, chunks):
        rg = rg_ref[0, rows, :]
        gate = rg / (1.0 + jnp.exp2(rg * -LOG2E))
        for h, hs in enumerate(heads):
            o = d["inter"][h] + jnp.dot(d["a"][h], d["v"][:, hs], preferred_element_type=F32)
            o_ref[0, rows, hs] = (_rms(o, g_ref[...]) * gate[:, hs]).astype(o_ref.dtype)


def _rec_kernel(rq_ref, rf_ref, ri_ref, rg_ref, llb_ref, l1m_ref, g_ref, o_ref, st_ref):
    @pl.when(pl.program_id(1) == 0)
    def _():
        st_ref[...] = jnp.zeros(st_ref.shape, F32)

    group = REC_GROUP * CHUNK

    def body(gi, carry):
        r0 = pl.multiple_of(gi * group, group)
        _rec_group([pl.ds(r0 + n * CHUNK, CHUNK) for n in range(REC_GROUP)],
                   rq_ref, rf_ref, ri_ref, rg_ref, llb_ref, l1m_ref, g_ref, o_ref, st_ref)
        return carry

    lax.fori_loop(0, rq_ref.shape[1] // group, body, 0)


def _recurrence(zr, log_lb, log_1mlb, g):
    b, s, cols4 = zr.shape
    w = cols4 // 4
    spec = lambda j: pl.BlockSpec((1, REC_TS, w), lambda bi, si, j=j: (bi, si, j))
    vec = lambda n: pl.BlockSpec((1, n), lambda bi, si: (0, 0))
    return pl.pallas_call(
        _rec_kernel,
        grid=(b, s // REC_TS),
        in_specs=[spec(0), spec(1), spec(2), spec(3), vec(w), vec(w), vec(LANES)],
        out_specs=pl.BlockSpec((1, REC_TS, w), lambda bi, si: (bi, si, 0)),
        out_shape=jax.ShapeDtypeStruct((b, s, w), BF16),
        scratch_shapes=[pltpu.VMEM((H_R, LANES, LANES), F32)],
        compiler_params=pltpu.CompilerParams(
            dimension_semantics=("parallel", "arbitrary"), vmem_limit_bytes=VMEM_LIMIT),
        name="hgrn2",
    )(zr, zr, zr, zr, log_lb, log_1mlb, g)


def _mlp_kernel(x_ref, oa_ref, or_ref, wo_ref, g2_ref, wu_ref, wd_ref, gf_ref, o_ref, u2_ref,
                *, final_norm):
    wa = oa_ref.shape[1]
    x = x_ref[...]
    x = x + jnp.dot(oa_ref[...], wo_ref[:wa, :], preferred_element_type=F32)
    x = x + jnp.dot(or_ref[...], wo_ref[wa:, :], preferred_element_type=F32)
    h2 = _rms(x, g2_ref[...]).astype(BF16)
    d_ff = wu_ref.shape[1]
    for f in range(d_ff // COL_TILE):
        cols = slice(f * COL_TILE, (f + 1) * COL_TILE)
        u = jnp.maximum(jnp.dot(h2, wu_ref[:, cols], preferred_element_type=F32), 0.0)
        u2_ref[:, cols] = (u * u).astype(BF16)
    x = x + jnp.dot(u2_ref[...], wd_ref[...], preferred_element_type=F32)
    if final_norm:
        x = _rms(x, gf_ref[...])
    o_ref[...] = x


def _mlp(x2d, oa, orr, wo, g2, wu, wd, gf, final_norm):
    n, d = x2d.shape
    wa, wr = oa.shape[1], orr.shape[1]
    d_ff = wu.shape[1]
    const = lambda shape: pl.BlockSpec(shape, lambda i: (0, 0), pipeline_mode=pl.Buffered(1))
    return pl.pallas_call(
        functools.partial(_mlp_kernel, final_norm=final_norm),
        grid=(n // ROW_TILE,),
        in_specs=[
            pl.BlockSpec((ROW_TILE, d), lambda i: (i, 0)),
            pl.BlockSpec((ROW_TILE, wa), lambda i: (i, 0)),
            pl.BlockSpec((ROW_TILE, wr), lambda i: (i, 0)),
            const((wa + wr, d)),
            const((1, d)),
            const((d, d_ff)),
            const((d_ff, d)),
            const((1, d)),
        ],
        out_specs=pl.BlockSpec((ROW_TILE, d), lambda i: (i, 0)),
        out_shape=jax.ShapeDtypeStruct((n, d), F32),
        scratch_shapes=[pltpu.VMEM((ROW_TILE, d_ff), BF16)],
        compiler_params=pltpu.CompilerParams(
            dimension_semantics=("parallel",), vmem_limit_bytes=VMEM_LIMIT),
        name="outproj_mlp",
    )(x2d, oa, orr, wo, g2, wu, wd, gf)


def kernel(x, norm1_g, w_in, lam_qk, attn_norm_g, lb_logits, hgrn_norm_g, w_out, norm2_g,
           w_up, w_down, rel_bias, final_g):
    b, s, d = x.shape
    depth = w_in.shape[0]
    att_w = attn_norm_g.shape[1] * H_A
    rec_w = hgrn_norm_g.shape[1] * H_R
    assert attn_norm_g.shape[1] == LANES and hgrn_norm_g.shape[1] == LANES
    assert w_in.shape[2] == 3 * att_w + 4 * rec_w
    assert (b * s) % ROW_TILE == 0 and s % ATT_TQ == 0 and s % REC_TS == 0

    lb = jnp.cumsum(jax.nn.softmax(lb_logits.astype(F32), axis=0), axis=0)
    lb = lb - lb[0:1]
    log_lb = jnp.log(lb) * LOG2E
    log_1mlb = jnp.log1p(-lb) * LOG2E

    bias = _bias_tiles(rel_bias)
    x2d = x.reshape(b * s, d)
    q_scale = (att_w // (2 * H_A)) ** -0.5 * LOG2E
    for l in range(depth):
        w = w_in[l].astype(BF16)
        zqt, zk, zvt, zr = _inproj(
            x2d, norm1_g[l][None], w[:, :att_w].T, w[:, att_w:2 * att_w],
            w[:, 2 * att_w:3 * att_w].T, w[:, 3 * att_w:], q_scale)

        lam_init = 0.8 - 0.6 * math.exp(-0.3 * l)
        lq = lam_qk[l].astype(F32)
        lam = jnp.exp(jnp.sum(lq[0] * lq[1])) - jnp.exp(jnp.sum(lq[2] * lq[3])) + lam_init
        oa = _attention(zqt, zk, zvt, bias, lam.reshape(1), attn_norm_g[l][None].astype(F32),
                        lam_init, b, s)
        orr = _recurrence(zr.reshape(b, s, 4 * rec_w), log_lb[l][None], log_1mlb[l][None],
                          hgrn_norm_g[l][None].astype(F32))
        x2d = _mlp(x2d, oa.reshape(b * s, att_w), orr.reshape(b * s, rec_w),
                   w_out[l].astype(BF16), norm2_g[l][None], w_up[l].astype(BF16),
                   w_down[l].astype(BF16), final_g[None], l == depth - 1)
    return x2d.reshape(b, s, d)
```

```python
import functools
import math

import jax
import jax.numpy as jnp
from jax import lax
from jax.experimental import pallas as pl
from jax.experimental.pallas import tpu as pltpu

CHUNK = 64
H_A = 4
H_R = 4
NUM_BUCKETS = 32
MAX_DISTANCE = 128
EPS = 1e-6
NEG_INF = -1e30

LANES = 128
ROW_TILE = 1024
COL_TILE = 512
ATT_TQ = 256
ATT_TK = 256
QK_AHEAD = 4
REC_TS = 1024
REC_GROUP = 8
SUB = 16
LOG2E = 1.4426950408889634
EXP2_CLAMP = 80.0 * LOG2E
ONES_ROWS = 16
V_ROWS = LANES + ONES_ROWS
VMEM_LIMIT = 56 * 1024 * 1024

F32 = jnp.float32
BF16 = jnp.bfloat16


def _rms(x, g):
    return x * lax.rsqrt(jnp.mean(x * x, axis=-1, keepdims=True) + EPS) * g


def _inproj_kernel(x_ref, g_ref, wqt_ref, wk_ref, wvt_ref, wr_ref,
                   zqt_ref, zk_ref, zvt_ref, zr_ref, *, q_scale):
    h = _rms(x_ref[...], g_ref[...]).astype(BF16)
    nt = (((1,), (1,)), ((), ()))
    zqt_ref[...] = (lax.dot_general(wqt_ref[...], h, nt, preferred_element_type=F32)
                    * q_scale).astype(BF16)
    zk_ref[...] = jnp.dot(h, wk_ref[...], preferred_element_type=F32).astype(BF16)
    zvt = lax.dot_general(wvt_ref[...], h, nt, preferred_element_type=F32).astype(BF16)
    ones = jnp.ones((ONES_ROWS, ATT_TK), BF16)
    for c in range(zvt_ref.shape[0]):
        for hd in range(H_A):
            zvt_ref[c, hd * V_ROWS:hd * V_ROWS + LANES, :] = (
                zvt[hd * LANES:(hd + 1) * LANES, c * ATT_TK:(c + 1) * ATT_TK])
            zvt_ref[c, hd * V_ROWS + LANES:(hd + 1) * V_ROWS, :] = ones
    for j in range(wr_ref.shape[1] // COL_TILE):
        cols = slice(j * COL_TILE, (j + 1) * COL_TILE)
        zr_ref[:, cols] = jnp.dot(h, wr_ref[:, cols], preferred_element_type=F32)


def _inproj(x2d, g, wqt, wk, wvt, wr, q_scale):
    n, d = x2d.shape
    att_w = wk.shape[1]
    rec_cols = wr.shape[1]
    const = lambda shape: pl.BlockSpec(shape, lambda i: (0,) * len(shape),
                                       pipeline_mode=pl.Buffered(1))
    return pl.pallas_call(
        functools.partial(_inproj_kernel, q_scale=q_scale),
        grid=(n // ROW_TILE,),
        in_specs=[
            pl.BlockSpec((ROW_TILE, d), lambda i: (i, 0)),
            const((1, d)), const((att_w, d)), const((d, att_w)), const((att_w, d)),
            const((d, rec_cols)),
        ],
        out_specs=[
            pl.BlockSpec((att_w, ROW_TILE), lambda i: (0, i)),
            pl.BlockSpec((ROW_TILE, att_w), lambda i: (i, 0)),
            pl.BlockSpec((ROW_TILE // ATT_TK, H_A * V_ROWS, ATT_TK), lambda i: (i, 0, 0)),
            pl.BlockSpec((ROW_TILE, rec_cols), lambda i: (i, 0)),
        ],
        out_shape=[
            jax.ShapeDtypeStruct((att_w, n), BF16),
            jax.ShapeDtypeStruct((n, att_w), BF16),
            jax.ShapeDtypeStruct((n // ATT_TK, H_A * V_ROWS, ATT_TK), BF16),
            jax.ShapeDtypeStruct((n, rec_cols), F32),
        ],
        compiler_params=pltpu.CompilerParams(
            dimension_semantics=("parallel",), vmem_limit_bytes=VMEM_LIMIT),
        name="inproj",
    )(x2d, g, wqt, wk, wvt, wr)


def _t5_bucket(rel):
    n_half = NUM_BUCKETS // 2
    max_exact = n_half // 2
    ret = jnp.where(rel > 0, n_half, 0)
    n = jnp.abs(rel)
    nf = jnp.maximum(n, 1).astype(jnp.float32)
    large = max_exact + (jnp.log(nf / max_exact) / math.log(MAX_DISTANCE / max_exact)
                         * (n_half - max_exact)).astype(jnp.int32)
    large = jnp.minimum(large, n_half - 1)
    return ret + jnp.where(n < max_exact, n, large)


def _bias_kernel(rb_ref, idx_ref, o_ref):
    h = pl.program_id(0)
    far = rb_ref[NUM_BUCKETS // 2 - 1, h]
    for t in range(2):
        idx = idx_ref[t]
        acc = jnp.zeros(idx.shape, F32)
        for b in range(NUM_BUCKETS):
            acc = jnp.where(idx == b, (rb_ref[b, h] - far) * LOG2E, acc)
        if t == 0:
            krow = lax.broadcasted_iota(jnp.int32, idx.shape, 0)
            qcol = lax.broadcasted_iota(jnp.int32, idx.shape, 1)
            acc = jnp.where(krow // CHUNK <= qcol // CHUNK, acc, NEG_INF)
        o_ref[0, t] = jnp.concatenate([acc, acc], axis=1)


def _bias_tiles(rel_bias):
    assert ATT_TQ == ATT_TK and ATT_TQ % MAX_DISTANCE == 0 and ATT_TQ % CHUNK == 0
    assert MAX_DISTANCE % LANES == 0
    kpos = jnp.arange(ATT_TK)[:, None]
    qpos = jnp.arange(ATT_TQ)[None, :]
    idx = jnp.stack([_t5_bucket(kpos - qpos), _t5_bucket(kpos - ATT_TK - qpos)]).astype(jnp.int32)
    return pl.pallas_call(
        _bias_kernel,
        grid=(H_A,),
        in_specs=[
            pl.BlockSpec(memory_space=pltpu.SMEM),
            pl.BlockSpec((2, ATT_TK, ATT_TQ), lambda h: (0, 0, 0)),
        ],
        out_specs=pl.BlockSpec((1, 2, ATT_TK, 2 * ATT_TQ), lambda h: (h, 0, 0, 0)),
        out_shape=jax.ShapeDtypeStruct((H_A, 2, ATT_TK, 2 * ATT_TQ), F32),
        name="bias_tiles",
    )(rel_bias.astype(F32), idx)


def _attn_kernel(lam_ref, qt_ref, qn_ref, k_ref, vt_ref, b_ref, g_ref, o_ref,
                 qq_ref, m_ref, acc_ref, ahead_ref, *, out_scale):
    qi = pl.program_id(1)
    tq = qt_ref.shape[1]
    half = LANES // 2
    zero = jnp.zeros((half, tq), BF16)

    def both_maps(q_ref, h):
        qt = q_ref[h * LANES:(h + 1) * LANES, :]
        return jnp.concatenate([jnp.concatenate([qt[:half], zero], axis=0),
                                jnp.concatenate([zero, qt[half:]], axis=0)], axis=1)

    for h in range(H_A):
        qq_ref[h] = both_maps(qt_ref, h)
    m_ref[...] = jnp.full(m_ref.shape, NEG_INF, F32)
    acc_ref[...] = jnp.zeros(acc_ref.shape, F32)

    def unit(u):
        h, mp = divmod(u, 2)
        return h, slice(mp * tq, (mp + 1) * tq)

    def scores(j, u, qq=None):
        h, lanes = unit(u)
        start = pl.multiple_of(j * ATT_TK, ATT_TK)
        kj = k_ref[0, pl.ds(start, ATT_TK), h * LANES:(h + 1) * LANES]
        rhs = qq_ref[h, :, lanes] if qq is None else qq[:, lanes]
        return jnp.dot(kj, rhs, preferred_element_type=F32)

    def biased(x, u, bias_tile):
        h, lanes = unit(u)
        if bias_tile is None:
            return x
        if bias_tile == 0:
            return x + b_ref[h, 0, :, lanes]
        z = ATT_TK - MAX_DISTANCE
        corner = x[z:, :MAX_DISTANCE] + b_ref[h, 1, z:, lanes.start:lanes.start + MAX_DISTANCE]
        return jnp.concatenate(
            [x[:z], jnp.concatenate([corner, x[z:, MAX_DISTANCE:]], axis=1)], axis=0)

    def update(j, u, s, bias_tile):
        h, lanes = unit(u)
        m_prev = m_ref[h, :, lanes]
        m_next = jnp.maximum(m_prev, jnp.max(biased(s, u, bias_tile), axis=0, keepdims=True))
        p = jnp.exp2(biased(s - m_next, u, bias_tile)).astype(BF16)
        alpha = jnp.exp2(m_prev - m_next)
        vj = vt_ref[j, h * V_ROWS:(h + 1) * V_ROWS, :]
        acc_ref[h, :, lanes] = alpha * acc_ref[h, :, lanes] + jnp.dot(
            vj, p, preferred_element_type=F32)
        m_ref[h, :, lanes] = m_next

    def run(blocks, next_block):
        items = [(j, u, bt) for j, bt in blocks for u in range(2 * H_A)]
        pending = [ahead_ref[n] for n in range(QK_AHEAD)]
        for n, (j, u, bt) in enumerate(items):
            k = n + QK_AHEAD
            if k < len(items):
                pending.append(scores(items[k][0], items[k][1]))
            elif next_block is not None:
                ahead_ref[k - len(items)] = scores(next_block, k - len(items))
            update(j, u, pending.pop(0), bt)

    @pl.when(qi == 0)
    def _():
        for n in range(QK_AHEAD):
            ahead_ref[n] = scores(0, n)

    n_far = jnp.maximum(qi - 1, 0)

    def far_quad(i, carry):
        run([(4 * i + n, None) for n in range(4)], 4 * i + 4)
        return carry

    lax.fori_loop(0, n_far // 4, far_quad, 0)
    quad_end = (n_far // 4) * 4

    @pl.when(n_far % 4 >= 2)
    def _():
        run([(quad_end, None), (quad_end + 1, None)], quad_end + 2)

    @pl.when(n_far % 2 == 1)
    def _():
        run([(n_far - 1, None)], n_far)

    @pl.when(qi >= 1)
    def _():
        run([(qi - 1, 1), (qi, 0)], None)

    @pl.when(qi == 0)
    def _():
        run([(qi, 0)], None)

    for n in range(QK_AHEAD):
        ahead_ref[n] = scores(0, n, both_maps(qn_ref, n // 2))

    gs = g_ref[...] * out_scale
    for h in range(H_A):
        acc = acc_ref[h]
        inv = 1.0 / acc[LANES:LANES + 1]
        ot = acc[:LANES, :tq] * inv[:, :tq] - acc[:LANES, tq:] * (lam_ref[0] * inv[:, tq:])
        ot = ot * lax.rsqrt(jnp.mean(ot * ot, axis=0, keepdims=True) + EPS)
        o_ref[0, :, h * LANES:(h + 1) * LANES] = (jnp.transpose(ot) * gs).astype(o_ref.dtype)


def _attention(zqt, zk, zvt, bias, lam, g, lam_init, b, s):
    att_w = zk.shape[1]
    nq = s // ATT_TQ
    nk = s // ATT_TK
    kernel = functools.partial(_attn_kernel, out_scale=1.0 - lam_init)
    return pl.pallas_call(
        kernel,
        grid=(b, nq),
        in_specs=[
            pl.BlockSpec(memory_space=pltpu.SMEM),
            pl.BlockSpec((att_w, ATT_TQ), lambda bi, i: (0, bi * nq + i)),
            pl.BlockSpec((att_w, ATT_TQ), lambda bi, i: (0, bi * nq + jnp.minimum(i + 1, nq - 1))),
            pl.BlockSpec((1, s, att_w), lambda bi, i: (bi, 0, 0)),
            pl.BlockSpec((nk, H_A * V_ROWS, ATT_TK), lambda bi, i: (bi, 0, 0)),
            pl.BlockSpec((H_A, 2, ATT_TK, 2 * ATT_TQ), lambda bi, i: (0, 0, 0, 0),
                         pipeline_mode=pl.Buffered(1)),
            pl.BlockSpec((1, LANES), lambda bi, i: (0, 0)),
        ],
        out_specs=pl.BlockSpec((1, ATT_TQ, att_w), lambda bi, i: (bi, i, 0)),
        out_shape=jax.ShapeDtypeStruct((b, s, att_w), BF16),
        scratch_shapes=[
            pltpu.VMEM((H_A, LANES, 2 * ATT_TQ), BF16),
            pltpu.VMEM((H_A, 1, 2 * ATT_TQ), F32),
            pltpu.VMEM((H_A, V_ROWS, 2 * ATT_TQ), F32),
            pltpu.VMEM((QK_AHEAD, ATT_TK, ATT_TQ), F32),
        ],
        compiler_params=pltpu.CompilerParams(
            dimension_semantics=("parallel", "arbitrary"), vmem_limit_bytes=VMEM_LIMIT),
        name="diff_attention",
    )(lam, zqt, zqt, zk.reshape(b, s, att_w), zvt, bias, g)


def _cumsum_rows(g, tri3):
    hi = g.astype(BF16)
    r1 = g - hi.astype(F32)
    mid = r1.astype(BF16)
    lo = (r1 - mid.astype(F32)).astype(BF16)
    return jnp.dot(tri3, jnp.concatenate([hi, mid, lo], axis=0), preferred_element_type=F32)


def _rec_group(row_slices, rq_ref, rf_ref, ri_ref, rg_ref, llb_ref, l1m_ref, g_ref, o_ref,
               st_ref):
    c = CHUNK
    nt = (((1,), (1,)), ((), ()))
    tn = (((0,), (0,)), ((), ()))
    heads = [slice(h * LANES, (h + 1) * LANES) for h in range(H_R)]
    row = lax.broadcasted_iota(jnp.int32, (c, c), 0)
    col = lax.broadcasted_iota(jnp.int32, (c, c), 1)
    causal = col <= row
    tri = causal.astype(BF16)
    tri3 = jnp.concatenate([tri, tri, tri], axis=1)
    log_lb = llb_ref[...]
    log_1mlb = l1m_ref[...]

    chunks = []
    for rows in row_slices:
        rq = rq_ref[0, rows, :]
        rf2 = rf_ref[0, rows, :] * LOG2E
        ls = jnp.minimum(rf2, 0.0) - jnp.log2(1.0 + jnp.exp2(-jnp.abs(rf2)))
        bterm = log_1mlb + ls
        g = jnp.maximum(log_lb, bterm) + jnp.log2(1.0 + jnp.exp2(-jnp.abs(log_lb - bterm)))
        chunks.append(dict(g=g, kin=jnp.exp2(bterm - rf2),
                           q=rq / (1.0 + jnp.exp2(rq * -LOG2E)),
                           v=ri_ref[0, rows, :].astype(BF16)))
    for d in chunks:
        d["bc"] = _cumsum_rows(d["g"], tri3)

    for d in chunks:
        bc, q, kin = d["bc"], d["q"], d["kin"]
        blast = bc[c - 1:c]
        d["qe"] = (q * jnp.exp2(bc)).astype(BF16)
        d["ke"] = (kin * jnp.exp2(blast - bc)).astype(BF16)
        d["dl"] = jnp.exp2(blast)
        d["lhs"], d["rhs"] = [], []
        for i in range(c // SUB):
            lo, hi = i * SUB, (i + 1) * SUB
            mid = lo + SUB // 2
            u = jnp.clip(bc[lo:hi] - bc[mid:mid + 1], -EXP2_CLAMP, EXP2_CLAMP)
            qd = (q[lo:hi] * jnp.exp2(u)).astype(BF16)
            kd = (kin[lo:hi] * jnp.exp2(-u)).astype(BF16)
            kd_pad = jnp.concatenate(
                ([jnp.zeros((lo, kd.shape[1]), BF16)] if lo else []) + [kd]
                + ([jnp.zeros((c - hi, kd.shape[1]), BF16)] if c - hi else []), axis=0)
            if i == 0:
                d["lhs"].append((qd,))
                d["rhs"].append((kd_pad,))
            else:
                ref = bc[lo:lo + 1]
                qo = (q[lo:hi] * jnp.exp2(bc[lo:hi] - ref)).astype(BF16)
                ko = (kin[:lo] * jnp.exp2(ref - bc[:lo])).astype(BF16)
                ko_pad = jnp.concatenate([ko, jnp.zeros((c - lo, ko.shape[1]), BF16)], axis=0)
                d["lhs"].append((qo, qd))
                d["rhs"].append((ko_pad, kd_pad))

    for d in chunks:
        d["inc"] = [lax.dot_general(d["v"][:, hs], d["ke"][:, hs], tn, preferred_element_type=F32)
                    for hs in heads]
    for d in chunks:
        d["a"] = []
        for hs in heads:
            a_rows = [lax.dot_general(jnp.concatenate([p[:, hs] for p in lhs], axis=1),
                                      jnp.concatenate([p[:, hs] for p in rhs], axis=1),
                                      nt, preferred_element_type=F32)
                      for lhs, rhs in zip(d["lhs"], d["rhs"])]
            d["a"].append(jnp.where(causal, jnp.concatenate(a_rows, axis=0), 0.0).astype(BF16))
    states = [st_ref[h] for h in range(H_R)]
    for d in chunks:
        d["inter"] = [lax.dot_general(d["qe"][:, hs], states[h].astype(BF16), nt,
                                      preferred_element_type=F32)
                      for h, hs in enumerate(heads)]
        states = [states[h] * d["dl"][:, hs] + d["inc"][h] for h, hs in enumerate(heads)]
    for h in range(H_R):
        st_ref[h] = states[h]
    for rows, d in zip(row_slices, chunks):
        rg = rg_ref[0, rows, :]
        gate = rg / (1.0 + jnp.exp2(rg * -LOG2E))
        for h, hs in enumerate(heads):
            o = d["inter"][h] + jnp.dot(d["a"][h], d["v"][:, hs], preferred_element_type=F32)
            o_ref[0, rows, hs] = (_rms(o, g_ref[...]) * gate[:, hs]).astype(o_ref.dtype)


def _rec_kernel(rq_ref, rf_ref, ri_ref, rg_ref, llb_ref, l1m_ref, g_ref, o_ref, st_ref):
    @pl.when(pl.program_id(1) == 0)
    def _():
        st_ref[...] = jnp.zeros(st_ref.shape, F32)

    group = REC_GROUP * CHUNK

    def body(gi, carry):
        r0 = pl.multiple_of(gi * group, group)
        _rec_group([pl.ds(r0 + n * CHUNK, CHUNK) for n in range(REC_GROUP)],
                   rq_ref, rf_ref, ri_ref, rg_ref, llb_ref, l1m_ref, g_ref, o_ref, st_ref)
        return carry

    lax.fori_loop(0, rq_ref.shape[1] // group, body, 0)


def _recurrence(zr, log_lb, log_1mlb, g):
    b, s, cols4 = zr.shape
    w = cols4 // 4
    spec = lambda j: pl.BlockSpec((1, REC_TS, w), lambda bi, si, j=j: (bi, si, j))
    vec = lambda n: pl.BlockSpec((1, n), lambda bi, si: (0, 0))
    return pl.pallas_call(
        _rec_kernel,
        grid=(b, s // REC_TS),
        in_specs=[spec(0), spec(1), spec(2), spec(3), vec(w), vec(w), vec(LANES)],
        out_specs=pl.BlockSpec((1, REC_TS, w), lambda bi, si: (bi, si, 0)),
        out_shape=jax.ShapeDtypeStruct((b, s, w), BF16),
        scratch_shapes=[pltpu.VMEM((H_R, LANES, LANES), F32)],
        compiler_params=pltpu.CompilerParams(
            dimension_semantics=("parallel", "arbitrary"), vmem_limit_bytes=VMEM_LIMIT),
        name="hgrn2",
    )(zr, zr, zr, zr, log_lb, log_1mlb, g)


def _mlp_kernel(x_ref, oa_ref, or_ref, wo_ref, g2_ref, wu_ref, wd_ref, gf_ref, o_ref, u2_ref,
                *, final_norm):
    wa = oa_ref.shape[1]
    x = x_ref[...]
    x = x + jnp.dot(oa_ref[...], wo_ref[:wa, :], preferred_element_type=F32)
    x = x + jnp.dot(or_ref[...], wo_ref[wa:, :], preferred_element_type=F32)
    h2 = _rms(x, g2_ref[...]).astype(BF16)
    d_ff = wu_ref.shape[1]
    for f in range(d_ff // COL_TILE):
        cols = slice(f * COL_TILE, (f + 1) * COL_TILE)
        u = jnp.maximum(jnp.dot(h2, wu_ref[:, cols], preferred_element_type=F32), 0.0)
        u2_ref[:, cols] = (u * u).astype(BF16)
    x = x + jnp.dot(u2_ref[...], wd_ref[...], preferred_element_type=F32)
    if final_norm:
        x = _rms(x, gf_ref[...])
    o_ref[...] = x


def _mlp(x2d, oa, orr, wo, g2, wu, wd, gf, final_norm):
    n, d = x2d.shape
    wa, wr = oa.shape[1], orr.shape[1]
    d_ff = wu.shape[1]
    const = lambda shape: pl.BlockSpec(shape, lambda i: (0, 0), pipeline_mode=pl.Buffered(1))
    return pl.pallas_call(
        functools.partial(_mlp_kernel, final_norm=final_norm),
        grid=(n // ROW_TILE,),
        in_specs=[
            pl.BlockSpec((ROW_TILE, d), lambda i: (i, 0)),
            pl.BlockSpec((ROW_TILE, wa), lambda i: (i, 0)),
            pl.BlockSpec((ROW_TILE, wr), lambda i: (i, 0)),
            const((wa + wr, d)),
            const((1, d)),
            const((d, d_ff)),
            const((d_ff, d)),
            const((1, d)),
        ],
        out_specs=pl.BlockSpec((ROW_TILE, d), lambda i: (i, 0)),
        out_shape=jax.ShapeDtypeStruct((n, d), F32),
        scratch_shapes=[pltpu.VMEM((ROW_TILE, d_ff), BF16)],
        compiler_params=pltpu.CompilerParams(
            dimension_semantics=("parallel",), vmem_limit_bytes=VMEM_LIMIT),
        name="outproj_mlp",
    )(x2d, oa, orr, wo, g2, wu, wd, gf)


def kernel(x, norm1_g, w_in, lam_qk, attn_norm_g, lb_logits, hgrn_norm_g, w_out, norm2_g,
           w_up, w_down, rel_bias, final_g):
    b, s, d = x.shape
    depth = w_in.shape[0]
    att_w = attn_norm_g.shape[1] * H_A
    rec_w = hgrn_norm_g.shape[1] * H_R
    assert attn_norm_g.shape[1] == LANES and hgrn_norm_g.shape[1] == LANES
    assert w_in.shape[2] == 3 * att_w + 4 * rec_w
    assert (b * s) % ROW_TILE == 0 and s % ATT_TQ == 0 and s % REC_TS == 0

    lb = jnp.cumsum(jax.nn.softmax(lb_logits.astype(F32), axis=0), axis=0)
    lb = lb - lb[0:1]
    log_lb = jnp.log(lb) * LOG2E
    log_1mlb = jnp.log1p(-lb) * LOG2E

    bias = _bias_tiles(rel_bias)
    x2d = x.reshape(b * s, d)
    q_scale = (att_w // (2 * H_A)) ** -0.5 * LOG2E
    for l in range(depth):
        w = w_in[l].astype(BF16)
        zqt, zk, zvt, zr = _inproj(
            x2d, norm1_g[l][None], w[:, :att_w].T, w[:, att_w:2 * att_w],
            w[:, 2 * att_w:3 * att_w].T, w[:, 3 * att_w:], q_scale)

        lam_init = 0.8 - 0.6 * math.exp(-0.3 * l)
        lq = lam_qk[l].astype(F32)
        lam = jnp.exp(jnp.sum(lq[0] * lq[1])) - jnp.exp(jnp.sum(lq[2] * lq[3])) + lam_init
        oa = _attention(zqt, zk, zvt, bias, lam.reshape(1), attn_norm_g[l][None].astype(F32),
                        lam_init, b, s)
        orr = _recurrence(zr.reshape(b, s, 4 * rec_w), log_lb[l][None], log_1mlb[l][None],
                          hgrn_norm_g[l][None].astype(F32))
        x2d = _mlp(x2d, oa.reshape(b * s, att_w), orr.reshape(b * s, rec_w),
                   w_out[l].astype(BF16), norm2_g[l][None], w_up[l].astype(BF16),
                   w_down[l].astype(BF16), final_g[None], l == depth - 1)
    return x2d.reshape(b, s, d)
```

```python
import functools
import math

import jax
import jax.numpy as jnp
from jax import lax
from jax.experimental import pallas as pl
from jax.experimental.pallas import tpu as pltpu

CHUNK = 64
H_A = 4
H_R = 4
NUM_BUCKETS = 32
MAX_DISTANCE = 128
EPS = 1e-6
NEG_INF = -1e30

LANES = 128
ROW_TILE = 1024
COL_TILE = 512
ATT_TQ = 256
ATT_TK = 256
QK_AHEAD = 4
REC_TS = 1024
REC_GROUP = 4
SUB = 16
LOG2E = 1.4426950408889634
EXP2_CLAMP = 80.0 * LOG2E
ONES_ROWS = 16
V_ROWS = LANES + ONES_ROWS
VMEM_LIMIT = 56 * 1024 * 1024

F32 = jnp.float32
BF16 = jnp.bfloat16


def _rms(x, g):
    return x * lax.rsqrt(jnp.mean(x * x, axis=-1, keepdims=True) + EPS) * g


def _inproj_kernel(x_ref, g_ref, wqt_ref, wk_ref, wvt_ref, wr_ref,
                   zqt_ref, zk_ref, zvt_ref, zr_ref, *, q_scale):
    h = _rms(x_ref[...], g_ref[...]).astype(BF16)
    nt = (((1,), (1,)), ((), ()))
    zqt_ref[...] = (lax.dot_general(wqt_ref[...], h, nt, preferred_element_type=F32)
                    * q_scale).astype(BF16)
    zk_ref[...] = jnp.dot(h, wk_ref[...], preferred_element_type=F32).astype(BF16)
    zvt = lax.dot_general(wvt_ref[...], h, nt, preferred_element_type=F32).astype(BF16)
    ones = jnp.ones((ONES_ROWS, ATT_TK), BF16)
    for c in range(zvt_ref.shape[0]):
        for hd in range(H_A):
            zvt_ref[c, hd * V_ROWS:hd * V_ROWS + LANES, :] = (
                zvt[hd * LANES:(hd + 1) * LANES, c * ATT_TK:(c + 1) * ATT_TK])
            zvt_ref[c, hd * V_ROWS + LANES:(hd + 1) * V_ROWS, :] = ones
    for j in range(wr_ref.shape[1] // COL_TILE):
        cols = slice(j * COL_TILE, (j + 1) * COL_TILE)
        zr_ref[:, cols] = jnp.dot(h, wr_ref[:, cols], preferred_element_type=F32)


def _inproj(x2d, g, wqt, wk, wvt, wr, q_scale):
    n, d = x2d.shape
    att_w = wk.shape[1]
    rec_cols = wr.shape[1]
    const = lambda shape: pl.BlockSpec(shape, lambda i: (0,) * len(shape),
                                       pipeline_mode=pl.Buffered(1))
    return pl.pallas_call(
        functools.partial(_inproj_kernel, q_scale=q_scale),
        grid=(n // ROW_TILE,),
        in_specs=[
            pl.BlockSpec((ROW_TILE, d), lambda i: (i, 0)),
            const((1, d)), const((att_w, d)), const((d, att_w)), const((att_w, d)),
            const((d, rec_cols)),
        ],
        out_specs=[
            pl.BlockSpec((att_w, ROW_TILE), lambda i: (0, i)),
            pl.BlockSpec((ROW_TILE, att_w), lambda i: (i, 0)),
            pl.BlockSpec((ROW_TILE // ATT_TK, H_A * V_ROWS, ATT_TK), lambda i: (i, 0, 0)),
            pl.BlockSpec((ROW_TILE, rec_cols), lambda i: (i, 0)),
        ],
        out_shape=[
            jax.ShapeDtypeStruct((att_w, n), BF16),
            jax.ShapeDtypeStruct((n, att_w), BF16),
            jax.ShapeDtypeStruct((n // ATT_TK, H_A * V_ROWS, ATT_TK), BF16),
            jax.ShapeDtypeStruct((n, rec_cols), F32),
        ],
        compiler_params=pltpu.CompilerParams(
            dimension_semantics=("parallel",), vmem_limit_bytes=VMEM_LIMIT),
        name="inproj",
    )(x2d, g, wqt, wk, wvt, wr)


def _t5_bucket(rel):
    n_half = NUM_BUCKETS // 2
    max_exact = n_half // 2
    ret = jnp.where(rel > 0, n_half, 0)
    n = jnp.abs(rel)
    nf = jnp.maximum(n, 1).astype(jnp.float32)
    large = max_exact + (jnp.log(nf / max_exact) / math.log(MAX_DISTANCE / max_exact)
                         * (n_half - max_exact)).astype(jnp.int32)
    large = jnp.minimum(large, n_half - 1)
    return ret + jnp.where(n < max_exact, n, large)


def _bias_kernel(rb_ref, idx_ref, o_ref):
    h = pl.program_id(0)
    far = rb_ref[NUM_BUCKETS // 2 - 1, h]
    for t in range(2):
        idx = idx_ref[t]
        acc = jnp.zeros(idx.shape, F32)
        for b in range(NUM_BUCKETS):
            acc = jnp.where(idx == b, (rb_ref[b, h] - far) * LOG2E, acc)
        if t == 0:
            krow = lax.broadcasted_iota(jnp.int32, idx.shape, 0)
            qcol = lax.broadcasted_iota(jnp.int32, idx.shape, 1)
            acc = jnp.where(krow // CHUNK <= qcol // CHUNK, acc, NEG_INF)
        o_ref[0, t] = jnp.concatenate([acc, acc], axis=1)


def _bias_tiles(rel_bias):
    assert ATT_TQ == ATT_TK and ATT_TQ % MAX_DISTANCE == 0 and ATT_TQ % CHUNK == 0
    assert MAX_DISTANCE % LANES == 0
    kpos = jnp.arange(ATT_TK)[:, None]
    qpos = jnp.arange(ATT_TQ)[None, :]
    idx = jnp.stack([_t5_bucket(kpos - qpos), _t5_bucket(kpos - ATT_TK - qpos)]).astype(jnp.int32)
    return pl.pallas_call(
        _bias_kernel,
        grid=(H_A,),
        in_specs=[
            pl.BlockSpec(memory_space=pltpu.SMEM),
            pl.BlockSpec((2, ATT_TK, ATT_TQ), lambda h: (0, 0, 0)),
        ],
        out_specs=pl.BlockSpec((1, 2, ATT_TK, 2 * ATT_TQ), lambda h: (h, 0, 0, 0)),
        out_shape=jax.ShapeDtypeStruct((H_A, 2, ATT_TK, 2 * ATT_TQ), F32),
        name="bias_tiles",
    )(rel_bias.astype(F32), idx)


def _attn_kernel(lam_ref, qt_ref, qn_ref, k_ref, vt_ref, b_ref, g_ref, o_ref,
                 qq_ref, m_ref, acc_ref, ahead_ref, *, out_scale):
    qi = pl.program_id(1)
    tq = qt_ref.shape[1]
    half = LANES // 2
    zero = jnp.zeros((half, tq), BF16)

    def both_maps(q_ref, h):
        qt = q_ref[h * LANES:(h + 1) * LANES, :]
        return jnp.concatenate([jnp.concatenate([qt[:half], zero], axis=0),
                                jnp.concatenate([zero, qt[half:]], axis=0)], axis=1)

    for h in range(H_A):
        qq_ref[h] = both_maps(qt_ref, h)
    m_ref[...] = jnp.full(m_ref.shape, NEG_INF, F32)
    acc_ref[...] = jnp.zeros(acc_ref.shape, F32)

    def unit(u):
        h, mp = divmod(u, 2)
        return h, slice(mp * tq, (mp + 1) * tq)

    def scores(j, u, qq=None):
        h, lanes = unit(u)
        start = pl.multiple_of(j * ATT_TK, ATT_TK)
        kj = k_ref[0, pl.ds(start, ATT_TK), h * LANES:(h + 1) * LANES]
        rhs = qq_ref[h, :, lanes] if qq is None else qq[:, lanes]
        return jnp.dot(kj, rhs, preferred_element_type=F32)

    def biased(x, u, bias_tile):
        h, lanes = unit(u)
        if bias_tile is None:
            return x
        if bias_tile == 0:
            return x + b_ref[h, 0, :, lanes]
        z = ATT_TK - MAX_DISTANCE
        corner = x[z:, :MAX_DISTANCE] + b_ref[h, 1, z:, lanes.start:lanes.start + MAX_DISTANCE]
        return jnp.concatenate(
            [x[:z], jnp.concatenate([corner, x[z:, MAX_DISTANCE:]], axis=1)], axis=0)

    def update(j, u, s, bias_tile):
        h, lanes = unit(u)
        m_prev = m_ref[h, :, lanes]
        m_next = jnp.maximum(m_prev, jnp.max(biased(s, u, bias_tile), axis=0, keepdims=True))
        p = jnp.exp2(biased(s - m_next, u, bias_tile)).astype(BF16)
        alpha = jnp.exp2(m_prev - m_next)
        vj = vt_ref[j, h * V_ROWS:(h + 1) * V_ROWS, :]
        acc_ref[h, :, lanes] = alpha * acc_ref[h, :, lanes] + jnp.dot(
            vj, p, preferred_element_type=F32)
        m_ref[h, :, lanes] = m_next

    def run(blocks, next_block):
        items = [(j, u, bt) for j, bt in blocks for u in range(2 * H_A)]
        pending = [ahead_ref[n] for n in range(QK_AHEAD)]
        for n, (j, u, bt) in enumerate(items):
            k = n + QK_AHEAD
            if k < len(items):
                pending.append(scores(items[k][0], items[k][1]))
            elif next_block is not None:
                ahead_ref[k - len(items)] = scores(next_block, k - len(items))
            update(j, u, pending.pop(0), bt)

    @pl.when(qi == 0)
    def _():
        for n in range(QK_AHEAD):
            ahead_ref[n] = scores(0, n)

    n_far = jnp.maximum(qi - 1, 0)

    def far_quad(i, carry):
        run([(4 * i + n, None) for n in range(4)], 4 * i + 4)
        return carry

    lax.fori_loop(0, n_far // 4, far_quad, 0)
    quad_end = (n_far // 4) * 4

    @pl.when(n_far % 4 >= 2)
    def _():
        run([(quad_end, None), (quad_end + 1, None)], quad_end + 2)

    @pl.when(n_far % 2 == 1)
    def _():
        run([(n_far - 1, None)], n_far)

    @pl.when(qi >= 1)
    def _():
        run([(qi - 1, 1), (qi, 0)], None)

    @pl.when(qi == 0)
    def _():
        run([(qi, 0)], None)

    for n in range(QK_AHEAD):
        ahead_ref[n] = scores(0, n, both_maps(qn_ref, n // 2))

    gs = g_ref[...] * out_scale
    for h in range(H_A):
        acc = acc_ref[h]
        inv = 1.0 / acc[LANES:LANES + 1]
        ot = acc[:LANES, :tq] * inv[:, :tq] - acc[:LANES, tq:] * (lam_ref[0] * inv[:, tq:])
        ot = ot * lax.rsqrt(jnp.mean(ot * ot, axis=0, keepdims=True) + EPS)
        o_ref[0, :, h * LANES:(h + 1) * LANES] = (jnp.transpose(ot) * gs).astype(o_ref.dtype)


def _attention(zqt, zk, zvt, bias, lam, g, lam_init, b, s):
    att_w = zk.shape[1]
    nq = s // ATT_TQ
    nk = s // ATT_TK
    kernel = functools.partial(_attn_kernel, out_scale=1.0 - lam_init)
    return pl.pallas_call(
        kernel,
        grid=(b, nq),
        in_specs=[
            pl.BlockSpec(memory_space=pltpu.SMEM),
            pl.BlockSpec((att_w, ATT_TQ), lambda bi, i: (0, bi * nq + i)),
            pl.BlockSpec((att_w, ATT_TQ), lambda bi, i: (0, bi * nq + jnp.minimum(i + 1, nq - 1))),
            pl.BlockSpec((1, s, att_w), lambda bi, i: (bi, 0, 0)),
            pl.BlockSpec((nk, H_A * V_ROWS, ATT_TK), lambda bi, i: (bi, 0, 0)),
            pl.BlockSpec((H_A, 2, ATT_TK, 2 * ATT_TQ), lambda bi, i: (0, 0, 0, 0),
                         pipeline_mode=pl.Buffered(1)),
            pl.BlockSpec((1, LANES), lambda bi, i: (0, 0)),
        ],
        out_specs=pl.BlockSpec((1, ATT_TQ, att_w), lambda bi, i: (bi, i, 0)),
        out_shape=jax.ShapeDtypeStruct((b, s, att_w), BF16),
        scratch_shapes=[
            pltpu.VMEM((H_A, LANES, 2 * ATT_TQ), BF16),
            pltpu.VMEM((H_A, 1, 2 * ATT_TQ), F32),
            pltpu.VMEM((H_A, V_ROWS, 2 * ATT_TQ), F32),
            pltpu.VMEM((QK_AHEAD, ATT_TK, ATT_TQ), F32),
        ],
        compiler_params=pltpu.CompilerParams(
            dimension_semantics=("parallel", "arbitrary"), vmem_limit_bytes=VMEM_LIMIT),
        name="diff_attention",
    )(lam, zqt, zqt, zk.reshape(b, s, att_w), zvt, bias, g)


def _cumsum_rows(g, tri3):
    hi = g.astype(BF16)
    r1 = g - hi.astype(F32)
    mid = r1.astype(BF16)
    lo = (r1 - mid.astype(F32)).astype(BF16)
    return jnp.dot(tri3, jnp.concatenate([hi, mid, lo], axis=0), preferred_element_type=F32)


def _rec_group(row_slices, rq_ref, rf_ref, ri_ref, rg_ref, llb_ref, l1m_ref, g_ref, o_ref,
               st_ref):
    c = CHUNK
    nt = (((1,), (1,)), ((), ()))
    tn = (((0,), (0,)), ((), ()))
    heads = [slice(h * LANES, (h + 1) * LANES) for h in range(H_R)]
    row = lax.broadcasted_iota(jnp.int32, (c, c), 0)
    col = lax.broadcasted_iota(jnp.int32, (c, c), 1)
    causal = col <= row
    tri = causal.astype(BF16)
    tri3 = jnp.concatenate([tri, tri, tri], axis=1)
    log_lb = llb_ref[...]
    log_1mlb = l1m_ref[...]

    chunks = []
    for rows in row_slices:
        rq = rq_ref[0, rows, :]
        rf2 = rf_ref[0, rows, :] * LOG2E
        ls = jnp.minimum(rf2, 0.0) - jnp.log2(1.0 + jnp.exp2(-jnp.abs(rf2)))
        bterm = log_1mlb + ls
        g = jnp.maximum(log_lb, bterm) + jnp.log2(1.0 + jnp.exp2(-jnp.abs(log_lb - bterm)))
        chunks.append(dict(g=g, kin=jnp.exp2(bterm - rf2),
                           q=rq / (1.0 + jnp.exp2(rq * -LOG2E)),
                           v=ri_ref[0, rows, :].astype(BF16)))
    for d in chunks:
        d["bc"] = _cumsum_rows(d["g"], tri3)

    for d in chunks:
        bc, q, kin = d["bc"], d["q"], d["kin"]
        blast = bc[c - 1:c]
        d["qe"] = (q * jnp.exp2(bc)).astype(BF16)
        d["ke"] = (kin * jnp.exp2(blast - bc)).astype(BF16)
        d["dl"] = jnp.exp2(blast)
        d["lhs"], d["rhs"] = [], []
        for i in range(c // SUB):
            lo, hi = i * SUB, (i + 1) * SUB
            mid = lo + SUB // 2
            u = jnp.clip(bc[lo:hi] - bc[mid:mid + 1], -EXP2_CLAMP, EXP2_CLAMP)
            qd = (q[lo:hi] * jnp.exp2(u)).astype(BF16)
            kd = (kin[lo:hi] * jnp.exp2(-u)).astype(BF16)
            kd_pad = jnp.concatenate(
                ([jnp.zeros((lo, kd.shape[1]), BF16)] if lo else []) + [kd]
                + ([jnp.zeros((c - hi, kd.shape[1]), BF16)] if c - hi else []), axis=0)
            if i == 0:
                d["lhs"].append((qd,))
                d["rhs"].append((kd_pad,))
            else:
                ref = bc[lo:lo + 1]
                qo = (q[lo:hi] * jnp.exp2(bc[lo:hi] - ref)).astype(BF16)
                ko = (kin[:lo] * jnp.exp2(ref - bc[:lo])).astype(BF16)
                ko_pad = jnp.concatenate([ko, jnp.zeros((c - lo, ko.shape[1]), BF16)], axis=0)
                d["lhs"].append((qo, qd))
                d["rhs"].append((ko_pad, kd_pad))

    for d in chunks:
        d["inc"] = [lax.dot_general(d["v"][:, hs], d["ke"][:, hs], tn, preferred_element_type=F32)
                    for hs in heads]
    for d in chunks:
        d["a"] = []
        for hs in heads:
            a_rows = [lax.dot_general(jnp.concatenate([p[:, hs] for p in lhs], axis=1),
                                      jnp.concatenate([p[:, hs] for p in rhs], axis=1),
                                      nt, preferred_element_type=F32)
                      for lhs, rhs in zip(d["lhs"], d["rhs"])]
            d["a"].append(jnp.where(causal, jnp.concatenate(a_rows, axis=0), 0.0).astype(BF16))
    states = [st_ref[h] for h in range(H_R)]
    for d in chunks:
        d["inter"] = [lax.dot_general(d["qe"][:, hs], states[h].astype(BF16), nt,
                                      preferred_element_type=F32)
                      for h, hs in enumerate(heads)]
        states = [states[h] * d["dl"][:, hs] + d["inc"][h] for h, hs in enumerate(heads)]
    for h in range(H_R):
        st_ref[h] = states[h]
    for rows, d in zip(row_slices, chunks):
        rg = rg_ref[0, rows, :]
        gate = rg / (1.0 + jnp.exp2(rg * -LOG2E))
        for h, hs in enumerate(heads):
            o = d["inter"][h] + jnp.dot(d["a"][h], d["v"][:, hs], preferred_element_type=F32)
            o_ref[0, rows, hs] = (_rms(o, g_ref[...]) * gate[:, hs]).astype(o_ref.dtype)


def _rec_kernel(rq_ref, rf_ref, ri_ref, rg_ref, llb_ref, l1m_ref, g_ref, o_ref, st_ref):
    @pl.when(pl.program_id(1) == 0)
    def _():
        st_ref[...] = jnp.zeros(st_ref.shape, F32)

    group = REC_GROUP * CHUNK

    def body(gi, carry):
        r0 = pl.multiple_of(gi * group, group)
        _rec_group([pl.ds(r0 + n * CHUNK, CHUNK) for n in range(REC_GROUP)],
                   rq_ref, rf_ref, ri_ref, rg_ref, llb_ref, l1m_ref, g_ref, o_ref, st_ref)
        return carry

    lax.fori_loop(0, rq_ref.shape[1] // group, body, 0)


def _recurrence(zr, log_lb, log_1mlb, g):
    b, s, cols4 = zr.shape
    w = cols4 // 4
    spec = lambda j: pl.BlockSpec((1, REC_TS, w), lambda bi, si, j=j: (bi, si, j))
    vec = lambda n: pl.BlockSpec((1, n), lambda bi, si: (0, 0))
    return pl.pallas_call(
        _rec_kernel,
        grid=(b, s // REC_TS),
        in_specs=[spec(0), spec(1), spec(2), spec(3), vec(w), vec(w), vec(LANES)],
        out_specs=pl.BlockSpec((1, REC_TS, w), lambda bi, si: (bi, si, 0)),
        out_shape=jax.ShapeDtypeStruct((b, s, w), BF16),
        scratch_shapes=[pltpu.VMEM((H_R, LANES, LANES), F32)],
        compiler_params=pltpu.CompilerParams(
            dimension_semantics=("parallel", "arbitrary"), vmem_limit_bytes=VMEM_LIMIT),
        name="hgrn2",
    )(zr, zr, zr, zr, log_lb, log_1mlb, g)


def _mlp_kernel(x_ref, oa_ref, or_ref, wo_ref, g2_ref, wu_ref, wd_ref, gf_ref, o_ref, u2_ref,
                *, final_norm):
    wa = oa_ref.shape[1]
    x = x_ref[...]
    x = x + jnp.dot(oa_ref[...], wo_ref[:wa, :], preferred_element_type=F32)
    x = x + jnp.dot(or_ref[...], wo_ref[wa:, :], preferred_element_type=F32)
    h2 = _rms(x, g2_ref[...]).astype(BF16)
    d_ff = wu_ref.shape[1]
    for f in range(d_ff // COL_TILE):
        cols = slice(f * COL_TILE, (f + 1) * COL_TILE)
        u = jnp.maximum(jnp.dot(h2, wu_ref[:, cols], preferred_element_type=F32), 0.0)
        u2_ref[:, cols] = (u * u).astype(BF16)
    x = x + jnp.dot(u2_ref[...], wd_ref[...], preferred_element_type=F32)
    if final_norm:
        x = _rms(x, gf_ref[...])
    o_ref[...] = x


def _mlp(x2d, oa, orr, wo, g2, wu, wd, gf, final_norm):
    n, d = x2d.shape
    wa, wr = oa.shape[1], orr.shape[1]
    d_ff = wu.shape[1]
    const = lambda shape: pl.BlockSpec(shape, lambda i: (0, 0), pipeline_mode=pl.Buffered(1))
    return pl.pallas_call(
        functools.partial(_mlp_kernel, final_norm=final_norm),
        grid=(n // ROW_TILE,),
        in_specs=[
            pl.BlockSpec((ROW_TILE, d), lambda i: (i, 0)),
            pl.BlockSpec((ROW_TILE, wa), lambda i: (i, 0)),
            pl.BlockSpec((ROW_TILE, wr), lambda i: (i, 0)),
            const((wa + wr, d)),
            const((1, d)),
            const((d, d_ff)),
            const((d_ff, d)),
            const((1, d)),
        ],
        out_specs=pl.BlockSpec((ROW_TILE, d), lambda i: (i, 0)),
        out_shape=jax.ShapeDtypeStruct((n, d), F32),
        scratch_shapes=[pltpu.VMEM((ROW_TILE, d_ff), BF16)],
        compiler_params=pltpu.CompilerParams(
            dimension_semantics=("parallel",), vmem_limit_bytes=VMEM_LIMIT),
        name="outproj_mlp",
    )(x2d, oa, orr, wo, g2, wu, wd, gf)


def kernel(x, norm1_g, w_in, lam_qk, attn_norm_g, lb_logits, hgrn_norm_g, w_out, norm2_g,
           w_up, w_down, rel_bias, final_g):
    b, s, d = x.shape
    depth = w_in.shape[0]
    att_w = attn_norm_g.shape[1] * H_A
    rec_w = hgrn_norm_g.shape[1] * H_R
    assert attn_norm_g.shape[1] == LANES and hgrn_norm_g.shape[1] == LANES
    assert w_in.shape[2] == 3 * att_w + 4 * rec_w
    assert (b * s) % ROW_TILE == 0 and s % ATT_TQ == 0 and s % REC_TS == 0

    lb = jnp.cumsum(jax.nn.softmax(lb_logits.astype(F32), axis=0), axis=0)
    lb = lb - lb[0:1]
    log_lb = jnp.log(lb) * LOG2E
    log_1mlb = jnp.log1p(-lb) * LOG2E

    bias = _bias_tiles(rel_bias)
    x2d = x.reshape(b * s, d)
    q_scale = (att_w // (2 * H_A)) ** -0.5 * LOG2E
    for l in range(depth):
        w = w_in[l].astype(BF16)
        zqt, zk, zvt, zr = _inproj(
            x2d, norm1_g[l][None], w[:, :att_w].T, w[:, att_w:2 * att_w],
            w[:, 2 * att_w:3 * att_w].T, w[:, 3 * att_w:], q_scale)

        lam_init = 0.8 - 0.6 * math.exp(-0.3 * l)
        lq = lam_qk[l].astype(F32)
        lam = jnp.exp(jnp.sum(lq[0] * lq[1])) - jnp.exp(jnp.sum(lq[2] * lq[3])) + lam_init
        oa = _attention(zqt, zk, zvt, bias, lam.reshape(1), attn_norm_g[l][None].astype(F32),
                        lam_init, b, s)
        orr = _recurrence(zr.reshape(b, s, 4 * rec_w), log_lb[l][None], log_1mlb[l][None],
                          hgrn_norm_g[l][None].astype(F32))
        x2d = _mlp(x2d, oa.reshape(b * s, att_w), orr.reshape(b * s, rec_w),
                   w_out[l].astype(BF16), norm2_g[l][None], w_up[l].astype(BF16),
                   w_down[l].astype(BF16), final_g[None], l == depth - 1)
    return x2d.reshape(b, s, d)
```

```python
import functools
import math

import jax
import jax.numpy as jnp
from jax import lax
from jax.experimental import pallas as pl
from jax.experimental.pallas import tpu as pltpu

CHUNK = 64
H_A = 4
H_R = 4
NUM_BUCKETS = 32
MAX_DISTANCE = 128
EPS = 1e-6
NEG_INF = -1e30

LANES = 128
ROW_TILE = 1024
COL_TILE = 512
ATT_TQ = 256
ATT_TK = 256
QK_AHEAD = 4
REC_TS = 2048
REC_GROUP = 8
SUB = 16
LOG2E = 1.4426950408889634
EXP2_CLAMP = 80.0 * LOG2E
ONES_ROWS = 16
V_ROWS = LANES + ONES_ROWS
VMEM_LIMIT = 56 * 1024 * 1024

F32 = jnp.float32
BF16 = jnp.bfloat16


def _rms(x, g):
    return x * lax.rsqrt(jnp.mean(x * x, axis=-1, keepdims=True) + EPS) * g


def _inproj_kernel(x_ref, g_ref, wqt_ref, wk_ref, wvt_ref, wr_ref,
                   zqt_ref, zk_ref, zvt_ref, zr_ref, *, q_scale):
    h = _rms(x_ref[...], g_ref[...]).astype(BF16)
    nt = (((1,), (1,)), ((), ()))
    zqt_ref[...] = (lax.dot_general(wqt_ref[...], h, nt, preferred_element_type=F32)
                    * q_scale).astype(BF16)
    zk_ref[...] = jnp.dot(h, wk_ref[...], preferred_element_type=F32).astype(BF16)
    zvt = lax.dot_general(wvt_ref[...], h, nt, preferred_element_type=F32).astype(BF16)
    ones = jnp.ones((ONES_ROWS, ATT_TK), BF16)
    for c in range(zvt_ref.shape[0]):
        for hd in range(H_A):
            zvt_ref[c, hd * V_ROWS:hd * V_ROWS + LANES, :] = (
                zvt[hd * LANES:(hd + 1) * LANES, c * ATT_TK:(c + 1) * ATT_TK])
            zvt_ref[c, hd * V_ROWS + LANES:(hd + 1) * V_ROWS, :] = ones
    for j in range(wr_ref.shape[1] // COL_TILE):
        cols = slice(j * COL_TILE, (j + 1) * COL_TILE)
        zr_ref[:, cols] = jnp.dot(h, wr_ref[:, cols], preferred_element_type=F32)


def _inproj(x2d, g, wqt, wk, wvt, wr, q_scale):
    n, d = x2d.shape
    att_w = wk.shape[1]
    rec_cols = wr.shape[1]
    const = lambda shape: pl.BlockSpec(shape, lambda i: (0,) * len(shape),
                                       pipeline_mode=pl.Buffered(1))
    return pl.pallas_call(
        functools.partial(_inproj_kernel, q_scale=q_scale),
        grid=(n // ROW_TILE,),
        in_specs=[
            pl.BlockSpec((ROW_TILE, d), lambda i: (i, 0)),
            const((1, d)), const((att_w, d)), const((d, att_w)), const((att_w, d)),
            const((d, rec_cols)),
        ],
        out_specs=[
            pl.BlockSpec((att_w, ROW_TILE), lambda i: (0, i)),
            pl.BlockSpec((ROW_TILE, att_w), lambda i: (i, 0)),
            pl.BlockSpec((ROW_TILE // ATT_TK, H_A * V_ROWS, ATT_TK), lambda i: (i, 0, 0)),
            pl.BlockSpec((ROW_TILE, rec_cols), lambda i: (i, 0)),
        ],
        out_shape=[
            jax.ShapeDtypeStruct((att_w, n), BF16),
            jax.ShapeDtypeStruct((n, att_w), BF16),
            jax.ShapeDtypeStruct((n // ATT_TK, H_A * V_ROWS, ATT_TK), BF16),
            jax.ShapeDtypeStruct((n, rec_cols), F32),
        ],
        compiler_params=pltpu.CompilerParams(
            dimension_semantics=("parallel",), vmem_limit_bytes=VMEM_LIMIT),
        name="inproj",
    )(x2d, g, wqt, wk, wvt, wr)


def _t5_bucket(rel):
    n_half = NUM_BUCKETS // 2
    max_exact = n_half // 2
    ret = jnp.where(rel > 0, n_half, 0)
    n = jnp.abs(rel)
    nf = jnp.maximum(n, 1).astype(jnp.float32)
    large = max_exact + (jnp.log(nf / max_exact) / math.log(MAX_DISTANCE / max_exact)
                         * (n_half - max_exact)).astype(jnp.int32)
    large = jnp.minimum(large, n_half - 1)
    return ret + jnp.where(n < max_exact, n, large)


def _bias_kernel(rb_ref, idx_ref, o_ref):
    h = pl.program_id(0)
    far = rb_ref[NUM_BUCKETS // 2 - 1, h]
    for t in range(2):
        idx = idx_ref[t]
        acc = jnp.zeros(idx.shape, F32)
        for b in range(NUM_BUCKETS):
            acc = jnp.where(idx == b, (rb_ref[b, h] - far) * LOG2E, acc)
        if t == 0:
            krow = lax.broadcasted_iota(jnp.int32, idx.shape, 0)
            qcol = lax.broadcasted_iota(jnp.int32, idx.shape, 1)
            acc = jnp.where(krow // CHUNK <= qcol // CHUNK, acc, NEG_INF)
        o_ref[0, t] = jnp.concatenate([acc, acc], axis=1)


def _bias_tiles(rel_bias):
    assert ATT_TQ == ATT_TK and ATT_TQ % MAX_DISTANCE == 0 and ATT_TQ % CHUNK == 0
    assert MAX_DISTANCE % LANES == 0
    kpos = jnp.arange(ATT_TK)[:, None]
    qpos = jnp.arange(ATT_TQ)[None, :]
    idx = jnp.stack([_t5_bucket(kpos - qpos), _t5_bucket(kpos - ATT_TK - qpos)]).astype(jnp.int32)
    return pl.pallas_call(
        _bias_kernel,
        grid=(H_A,),
        in_specs=[
            pl.BlockSpec(memory_space=pltpu.SMEM),
            pl.BlockSpec((2, ATT_TK, ATT_TQ), lambda h: (0, 0, 0)),
        ],
        out_specs=pl.BlockSpec((1, 2, ATT_TK, 2 * ATT_TQ), lambda h: (h, 0, 0, 0)),
        out_shape=jax.ShapeDtypeStruct((H_A, 2, ATT_TK, 2 * ATT_TQ), F32),
        name="bias_tiles",
    )(rel_bias.astype(F32), idx)


def _attn_kernel(lam_ref, qt_ref, qn_ref, k_ref, vt_ref, b_ref, g_ref, o_ref,
                 qq_ref, m_ref, acc_ref, ahead_ref, *, out_scale):
    qi = pl.program_id(1)
    tq = qt_ref.shape[1]
    half = LANES // 2
    zero = jnp.zeros((half, tq), BF16)

    def both_maps(q_ref, h):
        qt = q_ref[h * LANES:(h + 1) * LANES, :]
        return jnp.concatenate([jnp.concatenate([qt[:half], zero], axis=0),
                                jnp.concatenate([zero, qt[half:]], axis=0)], axis=1)

    for h in range(H_A):
        qq_ref[h] = both_maps(qt_ref, h)
    m_ref[...] = jnp.full(m_ref.shape, NEG_INF, F32)
    acc_ref[...] = jnp.zeros(acc_ref.shape, F32)

    def unit(u):
        h, mp = divmod(u, 2)
        return h, slice(mp * tq, (mp + 1) * tq)

    def scores(j, u, qq=None):
        h, lanes = unit(u)
        start = pl.multiple_of(j * ATT_TK, ATT_TK)
        kj = k_ref[0, pl.ds(start, ATT_TK), h * LANES:(h + 1) * LANES]
        rhs = qq_ref[h, :, lanes] if qq is None else qq[:, lanes]
        return jnp.dot(kj, rhs, preferred_element_type=F32)

    def biased(x, u, bias_tile):
        h, lanes = unit(u)
        if bias_tile is None:
            return x
        if bias_tile == 0:
            return x + b_ref[h, 0, :, lanes]
        z = ATT_TK - MAX_DISTANCE
        corner = x[z:, :MAX_DISTANCE] + b_ref[h, 1, z:, lanes.start:lanes.start + MAX_DISTANCE]
        return jnp.concatenate(
            [x[:z], jnp.concatenate([corner, x[z:, MAX_DISTANCE:]], axis=1)], axis=0)

    def update(j, u, s, bias_tile):
        h, lanes = unit(u)
        m_prev = m_ref[h, :, lanes]
        m_next = jnp.maximum(m_prev, jnp.max(biased(s, u, bias_tile), axis=0, keepdims=True))
        p = jnp.exp2(biased(s - m_next, u, bias_tile)).astype(BF16)
        alpha = jnp.exp2(m_prev - m_next)
        vj = vt_ref[j, h * V_ROWS:(h + 1) * V_ROWS, :]
        acc_ref[h, :, lanes] = alpha * acc_ref[h, :, lanes] + jnp.dot(
            vj, p, preferred_element_type=F32)
        m_ref[h, :, lanes] = m_next

    def run(blocks, next_block):
        items = [(j, u, bt) for j, bt in blocks for u in range(2 * H_A)]
        pending = [ahead_ref[n] for n in range(QK_AHEAD)]
        for n, (j, u, bt) in enumerate(items):
            k = n + QK_AHEAD
            if k < len(items):
                pending.append(scores(items[k][0], items[k][1]))
            elif next_block is not None:
                ahead_ref[k - len(items)] = scores(next_block, k - len(items))
            update(j, u, pending.pop(0), bt)

    @pl.when(qi == 0)
    def _():
        for n in range(QK_AHEAD):
            ahead_ref[n] = scores(0, n)

    n_far = jnp.maximum(qi - 1, 0)

    def far_oct(i, carry):
        run([(8 * i + n, None) for n in range(8)], 8 * i + 8)
        return carry

    lax.fori_loop(0, n_far // 8, far_oct, 0)
    oct_end = (n_far // 8) * 8

    @pl.when(n_far % 8 >= 4)
    def _():
        run([(oct_end + n, None) for n in range(4)], oct_end + 4)

    quad_end = (n_far // 4) * 4

    @pl.when(n_far % 4 >= 2)
    def _():
        run([(quad_end, None), (quad_end + 1, None)], quad_end + 2)

    @pl.when(n_far % 2 == 1)
    def _():
        run([(n_far - 1, None)], n_far)

    @pl.when(qi >= 1)
    def _():
        run([(qi - 1, 1), (qi, 0)], None)

    @pl.when(qi == 0)
    def _():
        run([(qi, 0)], None)

    for n in range(QK_AHEAD):
        ahead_ref[n] = scores(0, n, both_maps(qn_ref, n // 2))

    gs = g_ref[...] * out_scale
    for h in range(H_A):
        acc = acc_ref[h]
        inv = 1.0 / acc[LANES:LANES + 1]
        ot = acc[:LANES, :tq] * inv[:, :tq] - acc[:LANES, tq:] * (lam_ref[0] * inv[:, tq:])
        ot = ot * lax.rsqrt(jnp.mean(ot * ot, axis=0, keepdims=True) + EPS)
        o_ref[0, :, h * LANES:(h + 1) * LANES] = (jnp.transpose(ot) * gs).astype(o_ref.dtype)


def _attention(zqt, zk, zvt, bias, lam, g, lam_init, b, s):
    att_w = zk.shape[1]
    nq = s // ATT_TQ
    nk = s // ATT_TK
    kernel = functools.partial(_attn_kernel, out_scale=1.0 - lam_init)
    return pl.pallas_call(
        kernel,
        grid=(b, nq),
        in_specs=[
            pl.BlockSpec(memory_space=pltpu.SMEM),
            pl.BlockSpec((att_w, ATT_TQ), lambda bi, i: (0, bi * nq + i)),
            pl.BlockSpec((att_w, ATT_TQ), lambda bi, i: (0, bi * nq + jnp.minimum(i + 1, nq - 1))),
            pl.BlockSpec((1, s, att_w), lambda bi, i: (bi, 0, 0)),
            pl.BlockSpec((nk, H_A * V_ROWS, ATT_TK), lambda bi, i: (bi, 0, 0)),
            pl.BlockSpec((H_A, 2, ATT_TK, 2 * ATT_TQ), lambda bi, i: (0, 0, 0, 0),
                         pipeline_mode=pl.Buffered(1)),
            pl.BlockSpec((1, LANES), lambda bi, i: (0, 0)),
        ],
        out_specs=pl.BlockSpec((1, ATT_TQ, att_w), lambda bi, i: (bi, i, 0)),
        out_shape=jax.ShapeDtypeStruct((b, s, att_w), BF16),
        scratch_shapes=[
            pltpu.VMEM((H_A, LANES, 2 * ATT_TQ), BF16),
            pltpu.VMEM((H_A, 1, 2 * ATT_TQ), F32),
            pltpu.VMEM((H_A, V_ROWS, 2 * ATT_TQ), F32),
            pltpu.VMEM((QK_AHEAD, ATT_TK, ATT_TQ), F32),
        ],
        compiler_params=pltpu.CompilerParams(
            dimension_semantics=("parallel", "arbitrary"), vmem_limit_bytes=VMEM_LIMIT),
        name="diff_attention",
    )(lam, zqt, zqt, zk.reshape(b, s, att_w), zvt, bias, g)


def _cumsum_rows(g, tri3):
    hi = g.astype(BF16)
    r1 = g - hi.astype(F32)
    mid = r1.astype(BF16)
    lo = (r1 - mid.astype(F32)).astype(BF16)
    return jnp.dot(tri3, jnp.concatenate([hi, mid, lo], axis=0), preferred_element_type=F32)


def _rec_group(row_slices, rq_ref, rf_ref, ri_ref, rg_ref, llb_ref, l1m_ref, g_ref, o_ref,
               st_ref):
    c = CHUNK
    nt = (((1,), (1,)), ((), ()))
    tn = (((0,), (0,)), ((), ()))
    heads = [slice(h * LANES, (h + 1) * LANES) for h in range(H_R)]
    row = lax.broadcasted_iota(jnp.int32, (c, c), 0)
    col = lax.broadcasted_iota(jnp.int32, (c, c), 1)
    causal = col <= row
    tri = causal.astype(BF16)
    tri3 = jnp.concatenate([tri, tri, tri], axis=1)
    log_lb = llb_ref[...]
    log_1mlb = l1m_ref[...]

    chunks = []
    for rows in row_slices:
        rq = rq_ref[0, rows, :]
        rf2 = rf_ref[0, rows, :] * LOG2E
        ls = jnp.minimum(rf2, 0.0) - jnp.log2(1.0 + jnp.exp2(-jnp.abs(rf2)))
        bterm = log_1mlb + ls
        g = jnp.maximum(log_lb, bterm) + jnp.log2(1.0 + jnp.exp2(-jnp.abs(log_lb - bterm)))
        chunks.append(dict(g=g, kin=jnp.exp2(bterm - rf2),
                           q=rq / (1.0 + jnp.exp2(rq * -LOG2E)),
                           v=ri_ref[0, rows, :].astype(BF16)))
    for d in chunks:
        d["bc"] = _cumsum_rows(d["g"], tri3)

    for d in chunks:
        bc, q, kin = d["bc"], d["q"], d["kin"]
        blast = bc[c - 1:c]
        d["qe"] = (q * jnp.exp2(bc)).astype(BF16)
        d["ke"] = (kin * jnp.exp2(blast - bc)).astype(BF16)
        d["dl"] = jnp.exp2(blast)
        d["lhs"], d["rhs"] = [], []
        for i in range(c // SUB):
            lo, hi = i * SUB, (i + 1) * SUB
            mid = lo + SUB // 2
            u = jnp.clip(bc[lo:hi] - bc[mid:mid + 1], -EXP2_CLAMP, EXP2_CLAMP)
            qd = (q[lo:hi] * jnp.exp2(u)).astype(BF16)
            kd = (kin[lo:hi] * jnp.exp2(-u)).astype(BF16)
            kd_pad = jnp.concatenate(
                ([jnp.zeros((lo, kd.shape[1]), BF16)] if lo else []) + [kd]
                + ([jnp.zeros((c - hi, kd.shape[1]), BF16)] if c - hi else []), axis=0)
            if i == 0:
                d["lhs"].append((qd,))
                d["rhs"].append((kd_pad,))
            else:
                ref = bc[lo:lo + 1]
                qo = (q[lo:hi] * jnp.exp2(bc[lo:hi] - ref)).astype(BF16)
                ko = (kin[:lo] * jnp.exp2(ref - bc[:lo])).astype(BF16)
                ko_pad = jnp.concatenate([ko, jnp.zeros((c - lo, ko.shape[1]), BF16)], axis=0)
                d["lhs"].append((qo, qd))
                d["rhs"].append((ko_pad, kd_pad))

    for d in chunks:
        d["inc"] = [lax.dot_general(d["v"][:, hs], d["ke"][:, hs], tn, preferred_element_type=F32)
                    for hs in heads]
    for d in chunks:
        d["a"] = []
        for hs in heads:
            a_rows = [lax.dot_general(jnp.concatenate([p[:, hs] for p in lhs], axis=1),
                                      jnp.concatenate([p[:, hs] for p in rhs], axis=1),
                                      nt, preferred_element_type=F32)
                      for lhs, rhs in zip(d["lhs"], d["rhs"])]
            d["a"].append(jnp.where(causal, jnp.concatenate(a_rows, axis=0), 0.0).astype(BF16))
    states = [st_ref[h] for h in range(H_R)]
    for d in chunks:
        d["inter"] = [lax.dot_general(d["qe"][:, hs], states[h].astype(BF16), nt,
                                      preferred_element_type=F32)
                      for h, hs in enumerate(heads)]
        states = [states[h] * d["dl"][:, hs] + d["inc"][h] for h, hs in enumerate(heads)]
    for h in range(H_R):
        st_ref[h] = states[h]
    for rows, d in zip(row_slices, chunks):
        rg = rg_ref[0, rows, :]
        gate = rg / (1.0 + jnp.exp2(rg * -LOG2E))
        for h, hs in enumerate(heads):
            o = d["inter"][h] + jnp.dot(d["a"][h], d["v"][:, hs], preferred_element_type=F32)
            o_ref[0, rows, hs] = (_rms(o, g_ref[...]) * gate[:, hs]).astype(o_ref.dtype)


def _rec_kernel(rq_ref, rf_ref, ri_ref, rg_ref, llb_ref, l1m_ref, g_ref, o_ref, st_ref):
    @pl.when(pl.program_id(1) == 0)
    def _():
        st_ref[...] = jnp.zeros(st_ref.shape, F32)

    group = REC_GROUP * CHUNK

    def body(gi, carry):
        r0 = pl.multiple_of(gi * group, group)
        _rec_group([pl.ds(r0 + n * CHUNK, CHUNK) for n in range(REC_GROUP)],
                   rq_ref, rf_ref, ri_ref, rg_ref, llb_ref, l1m_ref, g_ref, o_ref, st_ref)
        return carry

    lax.fori_loop(0, rq_ref.shape[1] // group, body, 0)


def _recurrence(zr, log_lb, log_1mlb, g):
    b, s, cols4 = zr.shape
    w = cols4 // 4
    spec = lambda j: pl.BlockSpec((1, REC_TS, w), lambda bi, si, j=j: (bi, si, j))
    vec = lambda n: pl.BlockSpec((1, n), lambda bi, si: (0, 0))
    return pl.pallas_call(
        _rec_kernel,
        grid=(b, s // REC_TS),
        in_specs=[spec(0), spec(1), spec(2), spec(3), vec(w), vec(w), vec(LANES)],
        out_specs=pl.BlockSpec((1, REC_TS, w), lambda bi, si: (bi, si, 0)),
        out_shape=jax.ShapeDtypeStruct((b, s, w), BF16),
        scratch_shapes=[pltpu.VMEM((H_R, LANES, LANES), F32)],
        compiler_params=pltpu.CompilerParams(
            dimension_semantics=("parallel", "arbitrary"), vmem_limit_bytes=VMEM_LIMIT),
        name="hgrn2",
    )(zr, zr, zr, zr, log_lb, log_1mlb, g)


def _mlp_kernel(x_ref, oa_ref, or_ref, wo_ref, g2_ref, wu_ref, wd_ref, gf_ref, o_ref, u2_ref,
                *, final_norm):
    wa = oa_ref.shape[1]
    x = x_ref[...]
    x = x + jnp.dot(oa_ref[...], wo_ref[:wa, :], preferred_element_type=F32)
    x = x + jnp.dot(or_ref[...], wo_ref[wa:, :], preferred_element_type=F32)
    h2 = _rms(x, g2_ref[...]).astype(BF16)
    d_ff = wu_ref.shape[1]
    for f in range(d_ff // COL_TILE):
        cols = slice(f * COL_TILE, (f + 1) * COL_TILE)
        u = jnp.maximum(jnp.dot(h2, wu_ref[:, cols], preferred_element_type=F32), 0.0)
        u2_ref[:, cols] = (u * u).astype(BF16)
    x = x + jnp.dot(u2_ref[...], wd_ref[...], preferred_element_type=F32)
    if final_norm:
        x = _rms(x, gf_ref[...])
    o_ref[...] = x


def _mlp(x2d, oa, orr, wo, g2, wu, wd, gf, final_norm):
    n, d = x2d.shape
    wa, wr = oa.shape[1], orr.shape[1]
    d_ff = wu.shape[1]
    const = lambda shape: pl.BlockSpec(shape, lambda i: (0, 0), pipeline_mode=pl.Buffered(1))
    return pl.pallas_call(
        functools.partial(_mlp_kernel, final_norm=final_norm),
        grid=(n // ROW_TILE,),
        in_specs=[
            pl.BlockSpec((ROW_TILE, d), lambda i: (i, 0)),
            pl.BlockSpec((ROW_TILE, wa), lambda i: (i, 0)),
            pl.BlockSpec((ROW_TILE, wr), lambda i: (i, 0)),
            const((wa + wr, d)),
            const((1, d)),
            const((d, d_ff)),
            const((d_ff, d)),
            const((1, d)),
        ],
        out_specs=pl.BlockSpec((ROW_TILE, d), lambda i: (i, 0)),
        out_shape=jax.ShapeDtypeStruct((n, d), F32),
        scratch_shapes=[pltpu.VMEM((ROW_TILE, d_ff), BF16)],
        compiler_params=pltpu.CompilerParams(
            dimension_semantics=("parallel",), vmem_limit_bytes=VMEM_LIMIT),
        name="outproj_mlp",
    )(x2d, oa, orr, wo, g2, wu, wd, gf)


def kernel(x, norm1_g, w_in, lam_qk, attn_norm_g, lb_logits, hgrn_norm_g, w_out, norm2_g,
           w_up, w_down, rel_bias, final_g):
    b, s, d = x.shape
    depth = w_in.shape[0]
    att_w = attn_norm_g.shape[1] * H_A
    rec_w = hgrn_norm_g.shape[1] * H_R
    assert attn_norm_g.shape[1] == LANES and hgrn_norm_g.shape[1] == LANES
    assert w_in.shape[2] == 3 * att_w + 4 * rec_w
    assert (b * s) % ROW_TILE == 0 and s % ATT_TQ == 0 and s % REC_TS == 0

    lb = jnp.cumsum(jax.nn.softmax(lb_logits.astype(F32), axis=0), axis=0)
    lb = lb - lb[0:1]
    log_lb = jnp.log(lb) * LOG2E
    log_1mlb = jnp.log1p(-lb) * LOG2E

    bias = _bias_tiles(rel_bias)
    x2d = x.reshape(b * s, d)
    q_scale = (att_w // (2 * H_A)) ** -0.5 * LOG2E
    for l in range(depth):
        w = w_in[l].astype(BF16)
        zqt, zk, zvt, zr = _inproj(
            x2d, norm1_g[l][None], w[:, :att_w].T, w[:, att_w:2 * att_w],
            w[:, 2 * att_w:3 * att_w].T, w[:, 3 * att_w:], q_scale)

        lam_init = 0.8 - 0.6 * math.exp(-0.3 * l)
        lq = lam_qk[l].astype(F32)
        lam = jnp.exp(jnp.sum(lq[0] * lq[1])) - jnp.exp(jnp.sum(lq[2] * lq[3])) + lam_init
        oa = _attention(zqt, zk, zvt, bias, lam.reshape(1), attn_norm_g[l][None].astype(F32),
                        lam_init, b, s)
        orr = _recurrence(zr.reshape(b, s, 4 * rec_w), log_lb[l][None], log_1mlb[l][None],
                          hgrn_norm_g[l][None].astype(F32))
        x2d = _mlp(x2d, oa.reshape(b * s, att_w), orr.reshape(b * s, rec_w),
                   w_out[l].astype(BF16), norm2_g[l][None], w_up[l].astype(BF16),
                   w_down[l].astype(BF16), final_g[None], l == depth - 1)
    return x2d.reshape(b, s, d)
```

```python
import functools
import math

import jax
import jax.numpy as jnp
from jax import lax
from jax.experimental import pallas as pl
from jax.experimental.pallas import tpu as pltpu

CHUNK = 64
H_A = 4
H_R = 4
NUM_BUCKETS = 32
MAX_DISTANCE = 128
EPS = 1e-6
NEG_INF = -1e30

LANES = 128
ROW_TILE = 1024
COL_TILE = 512
ATT_TQ = 256
ATT_TK = 256
QK_AHEAD = 4
REC_TS = 2048
REC_GROUP = 8
REC_HEADS = 2
SUB = 16
LOG2E = 1.4426950408889634
EXP2_CLAMP = 80.0 * LOG2E
ONES_ROWS = 16
V_ROWS = LANES + ONES_ROWS
VMEM_LIMIT = 56 * 1024 * 1024

F32 = jnp.float32
BF16 = jnp.bfloat16


def _rms(x, g):
    return x * lax.rsqrt(jnp.mean(x * x, axis=-1, keepdims=True) + EPS) * g


def _inproj_kernel(x_ref, g_ref, wqt_ref, wk_ref, wvt_ref, wr_ref,
                   zqt_ref, zk_ref, zvt_ref, zr_ref, *, q_scale):
    h = _rms(x_ref[...], g_ref[...]).astype(BF16)
    nt = (((1,), (1,)), ((), ()))
    zqt_ref[...] = (lax.dot_general(wqt_ref[...], h, nt, preferred_element_type=F32)
                    * q_scale).astype(BF16)
    zk_ref[...] = jnp.dot(h, wk_ref[...], preferred_element_type=F32).astype(BF16)
    zvt = lax.dot_general(wvt_ref[...], h, nt, preferred_element_type=F32).astype(BF16)
    ones = jnp.ones((ONES_ROWS, ATT_TK), BF16)
    for c in range(zvt_ref.shape[0]):
        for hd in range(H_A):
            zvt_ref[c, hd * V_ROWS:hd * V_ROWS + LANES, :] = (
                zvt[hd * LANES:(hd + 1) * LANES, c * ATT_TK:(c + 1) * ATT_TK])
            zvt_ref[c, hd * V_ROWS + LANES:(hd + 1) * V_ROWS, :] = ones
    for j in range(wr_ref.shape[1] // COL_TILE):
        cols = slice(j * COL_TILE, (j + 1) * COL_TILE)
        zr_ref[:, cols] = jnp.dot(h, wr_ref[:, cols], preferred_element_type=F32)


def _inproj(x2d, g, wqt, wk, wvt, wr, q_scale):
    n, d = x2d.shape
    att_w = wk.shape[1]
    rec_cols = wr.shape[1]
    const = lambda shape: pl.BlockSpec(shape, lambda i: (0,) * len(shape),
                                       pipeline_mode=pl.Buffered(1))
    return pl.pallas_call(
        functools.partial(_inproj_kernel, q_scale=q_scale),
        grid=(n // ROW_TILE,),
        in_specs=[
            pl.BlockSpec((ROW_TILE, d), lambda i: (i, 0)),
            const((1, d)), const((att_w, d)), const((d, att_w)), const((att_w, d)),
            const((d, rec_cols)),
        ],
        out_specs=[
            pl.BlockSpec((att_w, ROW_TILE), lambda i: (0, i)),
            pl.BlockSpec((ROW_TILE, att_w), lambda i: (i, 0)),
            pl.BlockSpec((ROW_TILE // ATT_TK, H_A * V_ROWS, ATT_TK), lambda i: (i, 0, 0)),
            pl.BlockSpec((ROW_TILE, rec_cols), lambda i: (i, 0)),
        ],
        out_shape=[
            jax.ShapeDtypeStruct((att_w, n), BF16),
            jax.ShapeDtypeStruct((n, att_w), BF16),
            jax.ShapeDtypeStruct((n // ATT_TK, H_A * V_ROWS, ATT_TK), BF16),
            jax.ShapeDtypeStruct((n, rec_cols), F32),
        ],
        compiler_params=pltpu.CompilerParams(
            dimension_semantics=("parallel",), vmem_limit_bytes=VMEM_LIMIT),
        name="inproj",
    )(x2d, g, wqt, wk, wvt, wr)


def _t5_bucket(rel):
    n_half = NUM_BUCKETS // 2
    max_exact = n_half // 2
    ret = jnp.where(rel > 0, n_half, 0)
    n = jnp.abs(rel)
    nf = jnp.maximum(n, 1).astype(jnp.float32)
    large = max_exact + (jnp.log(nf / max_exact) / math.log(MAX_DISTANCE / max_exact)
                         * (n_half - max_exact)).astype(jnp.int32)
    large = jnp.minimum(large, n_half - 1)
    return ret + jnp.where(n < max_exact, n, large)


def _bias_kernel(rb_ref, idx_ref, o_ref):
    h = pl.program_id(0)
    far = rb_ref[NUM_BUCKETS // 2 - 1, h]
    for t in range(2):
        idx = idx_ref[t]
        acc = jnp.zeros(idx.shape, F32)
        for b in range(NUM_BUCKETS):
            acc = jnp.where(idx == b, (rb_ref[b, h] - far) * LOG2E, acc)
        if t == 0:
            krow = lax.broadcasted_iota(jnp.int32, idx.shape, 0)
            qcol = lax.broadcasted_iota(jnp.int32, idx.shape, 1)
            acc = jnp.where(krow // CHUNK <= qcol // CHUNK, acc, NEG_INF)
        o_ref[0, t] = jnp.concatenate([acc, acc], axis=1)


def _bias_tiles(rel_bias):
    assert ATT_TQ == ATT_TK and ATT_TQ % MAX_DISTANCE == 0 and ATT_TQ % CHUNK == 0
    assert MAX_DISTANCE % LANES == 0
    kpos = jnp.arange(ATT_TK)[:, None]
    qpos = jnp.arange(ATT_TQ)[None, :]
    idx = jnp.stack([_t5_bucket(kpos - qpos), _t5_bucket(kpos - ATT_TK - qpos)]).astype(jnp.int32)
    return pl.pallas_call(
        _bias_kernel,
        grid=(H_A,),
        in_specs=[
            pl.BlockSpec(memory_space=pltpu.SMEM),
            pl.BlockSpec((2, ATT_TK, ATT_TQ), lambda h: (0, 0, 0)),
        ],
        out_specs=pl.BlockSpec((1, 2, ATT_TK, 2 * ATT_TQ), lambda h: (h, 0, 0, 0)),
        out_shape=jax.ShapeDtypeStruct((H_A, 2, ATT_TK, 2 * ATT_TQ), F32),
        name="bias_tiles",
    )(rel_bias.astype(F32), idx)


def _attn_kernel(lam_ref, qt_ref, qn_ref, k_ref, vt_ref, b_ref, g_ref, o_ref,
                 qq_ref, m_ref, acc_ref, ahead_ref, *, out_scale):
    qi = pl.program_id(1)
    tq = qt_ref.shape[1]
    half = LANES // 2
    zero = jnp.zeros((half, tq), BF16)

    def both_maps(q_ref, h):
        qt = q_ref[h * LANES:(h + 1) * LANES, :]
        return jnp.concatenate([jnp.concatenate([qt[:half], zero], axis=0),
                                jnp.concatenate([zero, qt[half:]], axis=0)], axis=1)

    for h in range(H_A):
        qq_ref[h] = both_maps(qt_ref, h)
    m_ref[...] = jnp.full(m_ref.shape, NEG_INF, F32)
    acc_ref[...] = jnp.zeros(acc_ref.shape, F32)

    def unit(u):
        h, mp = divmod(u, 2)
        return h, slice(mp * tq, (mp + 1) * tq)

    def scores(j, u, qq=None):
        h, lanes = unit(u)
        start = pl.multiple_of(j * ATT_TK, ATT_TK)
        kj = k_ref[0, pl.ds(start, ATT_TK), h * LANES:(h + 1) * LANES]
        rhs = qq_ref[h, :, lanes] if qq is None else qq[:, lanes]
        return jnp.dot(kj, rhs, preferred_element_type=F32)

    def biased(x, u, bias_tile):
        h, lanes = unit(u)
        if bias_tile is None:
            return x
        if bias_tile == 0:
            return x + b_ref[h, 0, :, lanes]
        z = ATT_TK - MAX_DISTANCE
        corner = x[z:, :MAX_DISTANCE] + b_ref[h, 1, z:, lanes.start:lanes.start + MAX_DISTANCE]
        return jnp.concatenate(
            [x[:z], jnp.concatenate([corner, x[z:, MAX_DISTANCE:]], axis=1)], axis=0)

    def update(j, u, s, bias_tile):
        h, lanes = unit(u)
        m_prev = m_ref[h, :, lanes]
        m_next = jnp.maximum(m_prev, jnp.max(biased(s, u, bias_tile), axis=0, keepdims=True))
        p = jnp.exp2(biased(s - m_next, u, bias_tile)).astype(BF16)
        alpha = jnp.exp2(m_prev - m_next)
        vj = vt_ref[j, h * V_ROWS:(h + 1) * V_ROWS, :]
        acc_ref[h, :, lanes] = alpha * acc_ref[h, :, lanes] + jnp.dot(
            vj, p, preferred_element_type=F32)
        m_ref[h, :, lanes] = m_next

    def run(blocks, next_block):
        items = [(j, u, bt) for j, bt in blocks for u in range(2 * H_A)]
        pending = [ahead_ref[n] for n in range(QK_AHEAD)]
        for n, (j, u, bt) in enumerate(items):
            k = n + QK_AHEAD
            if k < len(items):
                pending.append(scores(items[k][0], items[k][1]))
            elif next_block is not None:
                ahead_ref[k - len(items)] = scores(next_block, k - len(items))
            update(j, u, pending.pop(0), bt)

    @pl.when(qi == 0)
    def _():
        for n in range(QK_AHEAD):
            ahead_ref[n] = scores(0, n)

    n_far = jnp.maximum(qi - 1, 0)

    def far_oct(i, carry):
        run([(8 * i + n, None) for n in range(8)], 8 * i + 8)
        return carry

    lax.fori_loop(0, n_far // 8, far_oct, 0)
    oct_end = (n_far // 8) * 8

    @pl.when(n_far % 8 >= 4)
    def _():
        run([(oct_end + n, None) for n in range(4)], oct_end + 4)

    quad_end = (n_far // 4) * 4

    @pl.when(n_far % 4 >= 2)
    def _():
        run([(quad_end, None), (quad_end + 1, None)], quad_end + 2)

    @pl.when(n_far % 2 == 1)
    def _():
        run([(n_far - 1, None)], n_far)

    @pl.when(qi >= 1)
    def _():
        run([(qi - 1, 1), (qi, 0)], None)

    @pl.when(qi == 0)
    def _():
        run([(qi, 0)], None)

    for n in range(QK_AHEAD):
        ahead_ref[n] = scores(0, n, both_maps(qn_ref, n // 2))

    gs = g_ref[...] * out_scale
    for h in range(H_A):
        acc = acc_ref[h]
        inv = 1.0 / acc[LANES:LANES + 1]
        ot = acc[:LANES, :tq] * inv[:, :tq] - acc[:LANES, tq:] * (lam_ref[0] * inv[:, tq:])
        ot = ot * lax.rsqrt(jnp.mean(ot * ot, axis=0, keepdims=True) + EPS)
        o_ref[0, :, h * LANES:(h + 1) * LANES] = (jnp.transpose(ot) * gs).astype(o_ref.dtype)


def _attention(zqt, zk, zvt, bias, lam, g, lam_init, b, s):
    att_w = zk.shape[1]
    nq = s // ATT_TQ
    nk = s // ATT_TK
    kernel = functools.partial(_attn_kernel, out_scale=1.0 - lam_init)
    return pl.pallas_call(
        kernel,
        grid=(b, nq),
        in_specs=[
            pl.BlockSpec(memory_space=pltpu.SMEM),
            pl.BlockSpec((att_w, ATT_TQ), lambda bi, i: (0, bi * nq + i)),
            pl.BlockSpec((att_w, ATT_TQ), lambda bi, i: (0, bi * nq + jnp.minimum(i + 1, nq - 1))),
            pl.BlockSpec((1, s, att_w), lambda bi, i: (bi, 0, 0)),
            pl.BlockSpec((nk, H_A * V_ROWS, ATT_TK), lambda bi, i: (bi, 0, 0)),
            pl.BlockSpec((H_A, 2, ATT_TK, 2 * ATT_TQ), lambda bi, i: (0, 0, 0, 0),
                         pipeline_mode=pl.Buffered(1)),
            pl.BlockSpec((1, LANES), lambda bi, i: (0, 0)),
        ],
        out_specs=pl.BlockSpec((1, ATT_TQ, att_w), lambda bi, i: (bi, i, 0)),
        out_shape=jax.ShapeDtypeStruct((b, s, att_w), BF16),
        scratch_shapes=[
            pltpu.VMEM((H_A, LANES, 2 * ATT_TQ), BF16),
            pltpu.VMEM((H_A, 1, 2 * ATT_TQ), F32),
            pltpu.VMEM((H_A, V_ROWS, 2 * ATT_TQ), F32),
            pltpu.VMEM((QK_AHEAD, ATT_TK, ATT_TQ), F32),
        ],
        compiler_params=pltpu.CompilerParams(
            dimension_semantics=("parallel", "arbitrary"), vmem_limit_bytes=VMEM_LIMIT),
        name="diff_attention",
    )(lam, zqt, zqt, zk.reshape(b, s, att_w), zvt, bias, g)


def _cumsum_rows(g, tri3):
    hi = g.astype(BF16)
    r1 = g - hi.astype(F32)
    mid = r1.astype(BF16)
    lo = (r1 - mid.astype(F32)).astype(BF16)
    return jnp.dot(tri3, jnp.concatenate([hi, mid, lo], axis=0), preferred_element_type=F32)


def _rec_group(row_slices, h0, rq_ref, rf_ref, ri_ref, rg_ref, llb_ref, l1m_ref, g_ref, o_ref,
               st_ref):
    c = CHUNK
    nt = (((1,), (1,)), ((), ()))
    tn = (((0,), (0,)), ((), ()))
    lanes = slice(h0 * LANES, (h0 + REC_HEADS) * LANES)
    heads = [slice(h * LANES, (h + 1) * LANES) for h in range(REC_HEADS)]
    row = lax.broadcasted_iota(jnp.int32, (c, c), 0)
    col = lax.broadcasted_iota(jnp.int32, (c, c), 1)
    causal = col <= row
    tri = causal.astype(BF16)
    tri3 = jnp.concatenate([tri, tri, tri], axis=1)
    log_lb = llb_ref[:, lanes]
    log_1mlb = l1m_ref[:, lanes]

    chunks = []
    for rows in row_slices:
        rq = rq_ref[0, rows, lanes]
        rf2 = rf_ref[0, rows, lanes] * LOG2E
        ls = jnp.minimum(rf2, 0.0) - jnp.log2(1.0 + jnp.exp2(-jnp.abs(rf2)))
        bterm = log_1mlb + ls
        g = jnp.maximum(log_lb, bterm) + jnp.log2(1.0 + jnp.exp2(-jnp.abs(log_lb - bterm)))
        chunks.append(dict(g=g, kin=jnp.exp2(bterm - rf2),
                           q=rq / (1.0 + jnp.exp2(rq * -LOG2E)),
                           v=ri_ref[0, rows, lanes].astype(BF16)))
    for d in chunks:
        d["bc"] = _cumsum_rows(d["g"], tri3)

    for d in chunks:
        bc, q, kin = d["bc"], d["q"], d["kin"]
        blast = bc[c - 1:c]
        d["qe"] = (q * jnp.exp2(bc)).astype(BF16)
        d["ke"] = (kin * jnp.exp2(blast - bc)).astype(BF16)
        d["dl"] = jnp.exp2(blast)
        d["lhs"], d["rhs"] = [], []
        for i in range(c // SUB):
            lo, hi = i * SUB, (i + 1) * SUB
            mid = lo + SUB // 2
            u = jnp.clip(bc[lo:hi] - bc[mid:mid + 1], -EXP2_CLAMP, EXP2_CLAMP)
            qd = (q[lo:hi] * jnp.exp2(u)).astype(BF16)
            kd = (kin[lo:hi] * jnp.exp2(-u)).astype(BF16)
            kd_pad = jnp.concatenate(
                ([jnp.zeros((lo, kd.shape[1]), BF16)] if lo else []) + [kd]
                + ([jnp.zeros((c - hi, kd.shape[1]), BF16)] if c - hi else []), axis=0)
            if i == 0:
                d["lhs"].append((qd,))
                d["rhs"].append((kd_pad,))
            else:
                ref = bc[lo:lo + 1]
                qo = (q[lo:hi] * jnp.exp2(bc[lo:hi] - ref)).astype(BF16)
                ko = (kin[:lo] * jnp.exp2(ref - bc[:lo])).astype(BF16)
                ko_pad = jnp.concatenate([ko, jnp.zeros((c - lo, ko.shape[1]), BF16)], axis=0)
                d["lhs"].append((qo, qd))
                d["rhs"].append((ko_pad, kd_pad))

    for d in chunks:
        d["inc"] = [lax.dot_general(d["v"][:, hs], d["ke"][:, hs], tn, preferred_element_type=F32)
                    for hs in heads]
    for d in chunks:
        d["a"] = []
        for hs in heads:
            a_rows = [lax.dot_general(jnp.concatenate([p[:, hs] for p in lhs], axis=1),
                                      jnp.concatenate([p[:, hs] for p in rhs], axis=1),
                                      nt, preferred_element_type=F32)
                      for lhs, rhs in zip(d["lhs"], d["rhs"])]
            d["a"].append(jnp.where(causal, jnp.concatenate(a_rows, axis=0), 0.0).astype(BF16))
    states = [st_ref[h0 + h] for h in range(REC_HEADS)]
    for d in chunks:
        d["inter"] = [lax.dot_general(d["qe"][:, hs], states[h].astype(BF16), nt,
                                      preferred_element_type=F32)
                      for h, hs in enumerate(heads)]
        states = [states[h] * d["dl"][:, hs] + d["inc"][h] for h, hs in enumerate(heads)]
    for h in range(REC_HEADS):
        st_ref[h0 + h] = states[h]
    for rows, d in zip(row_slices, chunks):
        rg = rg_ref[0, rows, lanes]
        gate = rg / (1.0 + jnp.exp2(rg * -LOG2E))
        for h, hs in enumerate(heads):
            o = d["inter"][h] + jnp.dot(d["a"][h], d["v"][:, hs], preferred_element_type=F32)
            out_lanes = slice((h0 + h) * LANES, (h0 + h + 1) * LANES)
            o_ref[0, rows, out_lanes] = (_rms(o, g_ref[...]) * gate[:, hs]).astype(o_ref.dtype)


def _rec_kernel(rq_ref, rf_ref, ri_ref, rg_ref, llb_ref, l1m_ref, g_ref, o_ref, st_ref):
    @pl.when(pl.program_id(1) == 0)
    def _():
        st_ref[...] = jnp.zeros(st_ref.shape, F32)

    group = REC_GROUP * CHUNK

    def body(gi, carry):
        r0 = pl.multiple_of(gi * group, group)
        for h0 in range(0, H_R, REC_HEADS):
            _rec_group([pl.ds(r0 + n * CHUNK, CHUNK) for n in range(REC_GROUP)], h0,
                       rq_ref, rf_ref, ri_ref, rg_ref, llb_ref, l1m_ref, g_ref, o_ref, st_ref)
        return carry

    lax.fori_loop(0, rq_ref.shape[1] // group, body, 0)


def _recurrence(zr, log_lb, log_1mlb, g):
    b, s, cols4 = zr.shape
    w = cols4 // 4
    spec = lambda j: pl.BlockSpec((1, REC_TS, w), lambda bi, si, j=j: (bi, si, j))
    vec = lambda n: pl.BlockSpec((1, n), lambda bi, si: (0, 0))
    return pl.pallas_call(
        _rec_kernel,
        grid=(b, s // REC_TS),
        in_specs=[spec(0), spec(1), spec(2), spec(3), vec(w), vec(w), vec(LANES)],
        out_specs=pl.BlockSpec((1, REC_TS, w), lambda bi, si: (bi, si, 0)),
        out_shape=jax.ShapeDtypeStruct((b, s, w), BF16),
        scratch_shapes=[pltpu.VMEM((H_R, LANES, LANES), F32)],
        compiler_params=pltpu.CompilerParams(
            dimension_semantics=("parallel", "arbitrary"), vmem_limit_bytes=VMEM_LIMIT),
        name="hgrn2",
    )(zr, zr, zr, zr, log_lb, log_1mlb, g)


def _mlp_kernel(x_ref, oa_ref, or_ref, wo_ref, g2_ref, wu_ref, wd_ref, gf_ref, o_ref, u2_ref,
                *, final_norm):
    wa = oa_ref.shape[1]
    x = x_ref[...]
    x = x + jnp.dot(oa_ref[...], wo_ref[:wa, :], preferred_element_type=F32)
    x = x + jnp.dot(or_ref[...], wo_ref[wa:, :], preferred_element_type=F32)
    h2 = _rms(x, g2_ref[...]).astype(BF16)
    d_ff = wu_ref.shape[1]
    for f in range(d_ff // COL_TILE):
        cols = slice(f * COL_TILE, (f + 1) * COL_TILE)
        u = jnp.maximum(jnp.dot(h2, wu_ref[:, cols], preferred_element_type=F32), 0.0)
        u2_ref[:, cols] = (u * u).astype(BF16)
    x = x + jnp.dot(u2_ref[...], wd_ref[...], preferred_element_type=F32)
    if final_norm:
        x = _rms(x, gf_ref[...])
    o_ref[...] = x


def _mlp(x2d, oa, orr, wo, g2, wu, wd, gf, final_norm):
    n, d = x2d.shape
    wa, wr = oa.shape[1], orr.shape[1]
    d_ff = wu.shape[1]
    const = lambda shape: pl.BlockSpec(shape, lambda i: (0, 0), pipeline_mode=pl.Buffered(1))
    return pl.pallas_call(
        functools.partial(_mlp_kernel, final_norm=final_norm),
        grid=(n // ROW_TILE,),
        in_specs=[
            pl.BlockSpec((ROW_TILE, d), lambda i: (i, 0)),
            pl.BlockSpec((ROW_TILE, wa), lambda i: (i, 0)),
            pl.BlockSpec((ROW_TILE, wr), lambda i: (i, 0)),
            const((wa + wr, d)),
            const((1, d)),
            const((d, d_ff)),
            const((d_ff, d)),
            const((1, d)),
        ],
        out_specs=pl.BlockSpec((ROW_TILE, d), lambda i: (i, 0)),
        out_shape=jax.ShapeDtypeStruct((n, d), F32),
        scratch_shapes=[pltpu.VMEM((ROW_TILE, d_ff), BF16)],
        compiler_params=pltpu.CompilerParams(
            dimension_semantics=("parallel",), vmem_limit_bytes=VMEM_LIMIT),
        name="outproj_mlp",
    )(x2d, oa, orr, wo, g2, wu, wd, gf)


def kernel(x, norm1_g, w_in, lam_qk, attn_norm_g, lb_logits, hgrn_norm_g, w_out, norm2_g,
           w_up, w_down, rel_bias, final_g):
    b, s, d = x.shape
    depth = w_in.shape[0]
    att_w = attn_norm_g.shape[1] * H_A
    rec_w = hgrn_norm_g.shape[1] * H_R
    assert attn_norm_g.shape[1] == LANES and hgrn_norm_g.shape[1] == LANES
    assert w_in.shape[2] == 3 * att_w + 4 * rec_w
    assert (b * s) % ROW_TILE == 0 and s % ATT_TQ == 0 and s % REC_TS == 0

    lb = jnp.cumsum(jax.nn.softmax(lb_logits.astype(F32), axis=0), axis=0)
    lb = lb - lb[0:1]
    log_lb = jnp.log(lb) * LOG2E
    log_1mlb = jnp.log1p(-lb) * LOG2E

    bias = _bias_tiles(rel_bias)
    x2d = x.reshape(b * s, d)
    q_scale = (att_w // (2 * H_A)) ** -0.5 * LOG2E
    for l in range(depth):
        w = w_in[l].astype(BF16)
        zqt, zk, zvt, zr = _inproj(
            x2d, norm1_g[l][None], w[:, :att_w].T, w[:, att_w:2 * att_w],
            w[:, 2 * att_w:3 * att_w].T, w[:, 3 * att_w:], q_scale)

        lam_init = 0.8 - 0.6 * math.exp(-0.3 * l)
        lq = lam_qk[l].astype(F32)
        lam = jnp.exp(jnp.sum(lq[0] * lq[1])) - jnp.exp(jnp.sum(lq[2] * lq[3])) + lam_init
        oa = _attention(zqt, zk, zvt, bias, lam.reshape(1), attn_norm_g[l][None].astype(F32),
                        lam_init, b, s)
        orr = _recurrence(zr.reshape(b, s, 4 * rec_w), log_lb[l][None], log_1mlb[l][None],
                          hgrn_norm_g[l][None].astype(F32))
        x2d = _mlp(x2d, oa.reshape(b * s, att_w), orr.reshape(b * s, rec_w),
                   w_out[l].astype(BF16), norm2_g[l][None], w_up[l].astype(BF16),
                   w_down[l].astype(BF16), final_g[None], l == depth - 1)
    return x2d.reshape(b, s, d)
```
